```python
import math
import jax
import jax.numpy as jnp
from jax import lax
import numpy as np

D_MODEL = 1024
BATCH = 16
SEQ = 4096
DEPTH = 1

CHUNK = 64
Q_BLOCK = 128
ATT_HEADS = 8
HEAD_DIM = 64
ATT_WIDTH = ATT_HEADS * HEAD_DIM
LRU_WIDTH = D_MODEL - ATT_WIDTH
LRU_BLOCKS = 8
LRU_BLOCK_DIM = LRU_WIDTH // LRU_BLOCKS
CONV_WIDTH = 4
LRU_C = 8.0
D_FF = -(-8 * D_MODEL // (3 * 256)) * 256
IN_WIDTH = 3 * ATT_WIDTH + ATT_HEADS + 2 * LRU_WIDTH
NORM_EPS = 1e-6

kernel_name = "hymba_fox_rglru_swiglu_layer"


def rmsnorm(x, g):
    xf = x.astype(jnp.float32)
    y = xf * lax.rsqrt(jnp.mean(xf * xf, axis=-1, keepdims=True) + NORM_EPS)
    return (y * g.astype(jnp.float32)).astype(x.dtype)


def forgetting_attention(q, k, v, cum_logf):
    seq = q.shape[2]
    scale = 1.0 / math.sqrt(HEAD_DIM)
    outs = []
    for start in range(0, seq, Q_BLOCK):
        end = start + Q_BLOCK
        qb = q[:, :, start:end]
        kb = k[:, :, :end]
        vb = v[:, :, :end]
        s = jnp.einsum('bhqd,bhkd->bhqk', qb, kb).astype(jnp.float32) * scale
        s = s + (cum_logf[:, :, start:end, None] - cum_logf[:, :, None, :end])
        mask = jnp.arange(start, end)[:, None] >= jnp.arange(end)[None, :]
        s = jnp.where(mask[None, None], s, -jnp.inf)
        p = jax.nn.softmax(s, axis=-1).astype(vb.dtype)
        outs.append(jnp.einsum('bhqk,bhkd->bhqd', p, vb))
    return jnp.concatenate(outs, axis=2)


def causal_depthwise_conv(x, w, b):
    seq = x.shape[1]
    xp = jnp.pad(x, ((0, 0), (CONV_WIDTH - 1, 0), (0, 0)))
    y = b
    for j in range(CONV_WIDTH):
        y = y + xp[:, j:j + seq] * w[j]
    return y


def _lin_combine(c1, c2):
    a1, b1 = c1
    a2, b2 = c2
    return a1 * a2, a2 * b1 + b2


def rg_lru(x, w_a, b_a, w_x, b_x, lam):
    bsz, seq, ch = x.shape
    xb = x.reshape(bsz, seq, LRU_BLOCKS, LRU_BLOCK_DIM)
    gate_a = jnp.einsum('bsnd,nde->bsne', xb, w_a).reshape(bsz, seq, ch) + b_a
    gate_x = jnp.einsum('bsnd,nde->bsne', xb, w_x).reshape(bsz, seq, ch) + b_x
    r = jax.nn.sigmoid(gate_a.astype(jnp.float32))
    i = jax.nn.sigmoid(gate_x.astype(jnp.float32))
    log_a = -LRU_C * r * jax.nn.softplus(-lam.astype(jnp.float32))
    a = jnp.exp(log_a)
    u = jnp.sqrt(-jnp.expm1(2.0 * log_a)) * (i * x.astype(jnp.float32))
    _, h = lax.associative_scan(_lin_combine, (a, u), axis=1)
    return h.astype(x.dtype)


def setup_inputs(seed: int = 0) -> dict:
    key = jax.random.key(seed)
    ks = jax.random.split(key, 20)
    f32 = jnp.float32
    nrm = lambda k, shape, s: jax.random.normal(k, shape, f32) * s
    x = jax.random.normal(ks[0], (BATCH, SEQ, D_MODEL), f32)
    norm1_g = 1.0 + nrm(ks[1], (DEPTH, D_MODEL), 0.02)
    w_in = nrm(ks[2], (DEPTH, D_MODEL, IN_WIDTH), D_MODEL ** -0.5)
    q_norm_g = 1.0 + nrm(ks[3], (DEPTH, HEAD_DIM), 0.02)
    k_norm_g = 1.0 + nrm(ks[4], (DEPTH, HEAD_DIM), 0.02)
    b_f = 2.0 + nrm(ks[5], (DEPTH, ATT_HEADS), 0.5)
    conv_w = nrm(ks[6], (DEPTH, CONV_WIDTH, LRU_WIDTH), CONV_WIDTH ** -0.5)
    conv_b = nrm(ks[7], (DEPTH, LRU_WIDTH), 0.01)
    w_a = nrm(ks[8], (DEPTH, LRU_BLOCKS, LRU_BLOCK_DIM, LRU_BLOCK_DIM), LRU_BLOCK_DIM ** -0.5)
    b_a = nrm(ks[9], (DEPTH, LRU_WIDTH), 0.01)
    w_x = nrm(ks[10], (DEPTH, LRU_BLOCKS, LRU_BLOCK_DIM, LRU_BLOCK_DIM), LRU_BLOCK_DIM ** -0.5)
    b_x = nrm(ks[11], (DEPTH, LRU_WIDTH), 0.01)
    ac = jax.random.uniform(ks[12], (DEPTH, LRU_WIDTH), f32, 0.9, 0.999)
    a0 = ac ** (1.0 / LRU_C)
    lam = jnp.log(a0) - jnp.log1p(-a0)
    attn_out_g = 1.0 + nrm(ks[13], (DEPTH, ATT_WIDTH), 0.02)
    lru_out_g = 1.0 + nrm(ks[14], (DEPTH, LRU_WIDTH), 0.02)
    w_out = nrm(ks[15], (DEPTH, D_MODEL, D_MODEL), D_MODEL ** -0.5)
    norm2_g = 1.0 + nrm(ks[16], (DEPTH, D_MODEL), 0.02)
    w_gate = nrm(ks[17], (DEPTH, D_MODEL, D_FF), D_MODEL ** -0.5)
    w_up = nrm(ks[18], (DEPTH, D_MODEL, D_FF), D_MODEL ** -0.5)
    w_down = nrm(ks[19], (DEPTH, D_FF, D_MODEL), D_FF ** -0.5)
    return {"x": x, "norm1_g": norm1_g, "w_in": w_in, "q_norm_g": q_norm_g,
            "k_norm_g": k_norm_g, "b_f": b_f, "conv_w": conv_w, "conv_b": conv_b,
            "w_a": w_a, "b_a": b_a, "w_x": w_x, "b_x": b_x, "lam": lam,
            "attn_out_g": attn_out_g, "lru_out_g": lru_out_g, "w_out": w_out,
            "norm2_g": norm2_g, "w_gate": w_gate, "w_up": w_up, "w_down": w_down}


def reference(x, norm1_g, w_in, q_norm_g, k_norm_g, b_f, conv_w, conv_b, w_a, b_a,
              w_x, b_x, lam, attn_out_g, lru_out_g, w_out, norm2_g, w_gate, w_up,
              w_down):
    bsz, seq, _ = x.shape
    split_at = [ATT_WIDTH, 2 * ATT_WIDTH, 3 * ATT_WIDTH, 3 * ATT_WIDTH + ATT_HEADS,
                3 * ATT_WIDTH + ATT_HEADS + LRU_WIDTH]
    for l in range(DEPTH):
        h = rmsnorm(x, norm1_g[l])
        proj = h @ w_in[l]
        q, k, v, f_logit, lru_x, lru_gate = jnp.split(proj, split_at, axis=-1)

        q = rmsnorm(q.reshape(bsz, seq, ATT_HEADS, HEAD_DIM), q_norm_g[l])
        k = rmsnorm(k.reshape(bsz, seq, ATT_HEADS, HEAD_DIM), k_norm_g[l])
        v = v.reshape(bsz, seq, ATT_HEADS, HEAD_DIM)
        q, k, v = (t.transpose(0, 2, 1, 3) for t in (q, k, v))
        log_f = jax.nn.log_sigmoid(f_logit.astype(jnp.float32) + b_f[l].astype(jnp.float32))
        cum_logf = jnp.cumsum(log_f, axis=1).transpose(0, 2, 1)
        att = forgetting_attention(q, k, v, cum_logf)
        att = att.transpose(0, 2, 1, 3).reshape(bsz, seq, ATT_WIDTH)

        xc = causal_depthwise_conv(lru_x, conv_w[l], conv_b[l])
        hr = rg_lru(xc, w_a[l], b_a[l], w_x[l], b_x[l], lam[l])
        rec = hr * jax.nn.gelu(lru_gate)

        mixed = jnp.concatenate([rmsnorm(att, attn_out_g[l]), rmsnorm(rec, lru_out_g[l])], axis=-1)
        x = x + mixed @ w_out[l]

        h2 = rmsnorm(x, norm2_g[l])
        x = x + (jax.nn.silu(h2 @ w_gate[l]) * (h2 @ w_up[l])) @ w_down[l]
    return x
```

```python
import functools
import math

import jax
import jax.numpy as jnp
from jax import lax
from jax.experimental import pallas as pl
from jax.experimental.pallas import tpu as pltpu

ATT_HEADS = 8
HEAD_DIM = 64
ATT_WIDTH = ATT_HEADS * HEAD_DIM
LRU_BLOCKS = 8
CONV_WIDTH = 4
LRU_C = 8.0
NORM_EPS = 1e-6

LANES = 128
SUBLANES = 8
HEAD_PAIR = 2 * HEAD_DIM
VMEM_LIMIT_BYTES = 56 * 1024 * 1024

F32 = jnp.float32
BF16 = jnp.bfloat16


def _dot(a, b):
    return jnp.dot(a, b, preferred_element_type=F32)


def _dot_nt(a, b):
    return lax.dot_general(a, b, (((1,), (1,)), ((), ())), preferred_element_type=F32)


def _rms(xf, g):
    return xf * lax.rsqrt(jnp.mean(xf * xf, axis=-1, keepdims=True) + NORM_EPS) * g


def _split3(x):
    hi = x.astype(BF16)
    r = x - hi.astype(F32)
    mid = r.astype(BF16)
    lo = (r - mid.astype(F32)).astype(BF16)
    return hi, mid, lo


def _proj_kernel(x_ref, g1_ref, wqkv_ref, wf_ref, wlru_ref, gq_ref, gk_ref, bf_ref,
                 hsum_ref, tri_ref,
                 q_ref, k_ref, v_ref, ft_ref, lx_ref, gate_ref, carry_ref):
    si = pl.program_id(1)

    @pl.when(si == 0)
    def _():
        carry_ref[...] = jnp.zeros_like(carry_ref)

    h = _rms(x_ref[0], g1_ref[...]).astype(BF16)
    hsum = hsum_ref[...]

    def head_norm(t, g):
        ms = _dot((t * t).astype(BF16), hsum) * (1.0 / HEAD_DIM)
        return t * lax.rsqrt(ms + NORM_EPS) * g

    aw = ATT_WIDTH
    q = _dot(h, wqkv_ref[:, 0:aw])
    q_ref[0] = (head_norm(q, gq_ref[...]) * (1.0 / math.sqrt(HEAD_DIM))).astype(BF16)
    k = _dot(h, wqkv_ref[:, aw:2 * aw])
    k_ref[0] = head_norm(k, gk_ref[...]).astype(BF16)
    v_ref[0] = _dot(h, wqkv_ref[:, 2 * aw:3 * aw]).astype(BF16)

    lw = lx_ref.shape[-1]
    lx_ref[0] = _dot(h, wlru_ref[:, 0:lw])
    gate_ref[0] = _dot(h, wlru_ref[:, lw:2 * lw])

    logf = jax.nn.log_sigmoid(_dot(h, wf_ref[...]) + bf_ref[...])
    tri = tri_ref[...]
    hi, mid, lo = _split3(logf)
    cum = _dot(tri, hi) + _dot(tri, mid) + _dot(tri, lo) + carry_ref[0:1, :]
    tm = cum.shape[0]
    carry_ref[...] = jnp.broadcast_to(cum[tm - 1:tm, :], carry_ref.shape)
    ft_ref[0] = cum.T[0:ATT_HEADS, :]


def _proj_call(x, g1, wqkv, wf, wlru, gq, gk, bf, hsum, tri, tm):
    B, S, D = x.shape
    aw = ATT_WIDTH
    lw = wlru.shape[1] // 2
    const = lambda shape: pl.BlockSpec(shape, lambda b, s: (0,) * len(shape))
    tok = lambda w: pl.BlockSpec((1, tm, w), lambda b, s: (b, s, 0))
    return pl.pallas_call(
        _proj_kernel,
        grid=(B, S // tm),
        in_specs=[tok(D), const((1, D)), const(wqkv.shape), const(wf.shape),
                  const(wlru.shape), const((1, aw)), const((1, aw)), const((1, LANES)),
                  const(hsum.shape), const(tri.shape)],
        out_specs=[tok(aw), tok(aw), tok(aw),
                   pl.BlockSpec((1, ATT_HEADS, tm), lambda b, s: (b, 0, s)),
                   tok(lw), tok(lw)],
        out_shape=[jax.ShapeDtypeStruct((B, S, aw), BF16)] * 3
        + [jax.ShapeDtypeStruct((B, ATT_HEADS, S), F32),
           jax.ShapeDtypeStruct((B, S, lw), F32),
           jax.ShapeDtypeStruct((B, S, lw), F32)],
        scratch_shapes=[pltpu.VMEM((SUBLANES, LANES), F32)],
        compiler_params=pltpu.CompilerParams(
            dimension_semantics=("arbitrary", "arbitrary"),
            vmem_limit_bytes=VMEM_LIMIT_BYTES),
        name="proj",
    )(x, g1, wqkv, wf, wlru, gq, gk, bf, hsum, tri)


def _attn_kernel(q_ref, k_ref, v_ref, f_ref, o_ref, *, tq):
    i = pl.program_id(2)
    q2 = q_ref[0]
    lane = lax.broadcasted_iota(jnp.int32, q2.shape, 1)
    first = lane < HEAD_DIM
    zero = jnp.zeros_like(q2)
    qs = jnp.concatenate([jnp.where(first, q2, zero), jnp.where(first, zero, q2)], axis=0)

    def step(j, carry, diag):
        kb = k_ref[0, pl.ds(j * tq, tq), :]
        vb = v_ref[0, pl.ds(j * tq, tq), :]
        s2 = _dot_nt(qs, kb)
        if diag:
            row = lax.broadcasted_iota(jnp.int32, (tq, tq), 0)
            col = lax.broadcasted_iota(jnp.int32, (tq, tq), 1)
            keep = row >= col
        new = []
        for hh in range(2):
            m, l, acc = carry[hh]
            s = s2[hh * tq:(hh + 1) * tq] - f_ref[0, 0, hh, pl.ds(j, 1), :]
            if diag:
                s = jnp.where(keep, s, -jnp.inf)
            m_new = jnp.maximum(m, jnp.max(s, axis=-1, keepdims=True))
            alpha = jnp.exp(m - m_new)
            p = jnp.exp(s - m_new)
            l = alpha * l + jnp.sum(p, axis=-1, keepdims=True)
            acc = alpha * acc + _dot(p.astype(BF16), vb)
            new.append((m_new, l, acc))
        return tuple(new)

    init = tuple((jnp.full((tq, 1), -jnp.inf, F32), jnp.zeros((tq, 1), F32),
                  jnp.zeros((tq, HEAD_PAIR), F32)) for _ in range(2))
    carry = lax.fori_loop(0, i, lambda j, c: step(j, c, False), init)
    (_, la, acca), (_, lb, accb) = step(i, carry, True)
    o_ref[0] = jnp.where(first, acca / la, accb / lb).astype(o_ref.dtype)


def _attn_call(q, k, v, ft, tq):
    B, S, aw = q.shape
    npair = aw // HEAD_PAIR
    f5 = ft.reshape(B, npair, 2, S // tq, tq)
    return pl.pallas_call(
        functools.partial(_attn_kernel, tq=tq),
        grid=(B, npair, S // tq),
        in_specs=[pl.BlockSpec((1, tq, HEAD_PAIR), lambda b, p, i: (b, i, p)),
                  pl.BlockSpec((1, S, HEAD_PAIR), lambda b, p, i: (b, 0, p)),
                  pl.BlockSpec((1, S, HEAD_PAIR), lambda b, p, i: (b, 0, p)),
                  pl.BlockSpec((1, 1, 2, S // tq, tq), lambda b, p, i: (b, p, 0, 0, 0))],
        out_specs=pl.BlockSpec((1, tq, HEAD_PAIR), lambda b, p, i: (b, i, p)),
        out_shape=jax.ShapeDtypeStruct((B, S, aw), BF16),
        compiler_params=pltpu.CompilerParams(
            dimension_semantics=("arbitrary", "arbitrary", "arbitrary"),
            vmem_limit_bytes=VMEM_LIMIT_BYTES),
        name="attn",
    )(q, k, v, f5)


def _lru_kernel(lx_ref, gate_ref, cw_ref, cb_ref, wa_ref, ba_ref, wx_ref, bx_ref,
                lam_ref, g_ref, o_ref, halo_ref, hc_ref, a_ref, u_ref, *, ts):
    si = pl.program_id(1)

    @pl.when(si == 0)
    def _():
        halo_ref[...] = jnp.zeros_like(halo_ref)
        hc_ref[...] = jnp.zeros_like(hc_ref)

    lx = lx_ref[0]
    ext = jnp.concatenate([halo_ref[...], lx], axis=0)
    halo_ref[...] = lx[ts - SUBLANES:ts, :]
    xc = cb_ref[...]
    for j in range(CONV_WIDTH):
        off = SUBLANES - (CONV_WIDTH - 1) + j
        xc = xc + ext[off:off + ts, :] * cw_ref[j:j + 1, :]

    xcb = xc.astype(BF16)
    r = jax.nn.sigmoid(_dot(xcb, wa_ref[...]) + ba_ref[...])
    gi = jax.nn.sigmoid(_dot(xcb, wx_ref[...]) + bx_ref[...])
    log_a = -LRU_C * r * jax.nn.softplus(-lam_ref[...])
    a = jnp.exp(log_a)
    u = jnp.sqrt(-jnp.tanh(log_a) * (1.0 + a * a)) * (gi * xc)

    rin = lax.broadcasted_iota(jnp.int32, a.shape, 0) % SUBLANES
    for sh in (1, 2, 4):
        ok = rin >= sh
        a_prev = jnp.where(ok, pltpu.roll(a, sh, axis=0), 1.0)
        u_prev = jnp.where(ok, pltpu.roll(u, sh, axis=0), 0.0)
        u = a * u_prev + u
        a = a * a_prev
    a_ref[...] = a
    u_ref[...] = u

    def group(g, hprev):
        rows = pl.ds(pl.multiple_of(g * SUBLANES, SUBLANES), SUBLANES)
        hg = a_ref[rows, :] * hprev + u_ref[rows, :]
        u_ref[rows, :] = hg
        return jnp.broadcast_to(hg[SUBLANES - 1:SUBLANES, :], hg.shape)

    hlast = lax.fori_loop(0, ts // SUBLANES, group, hc_ref[...], unroll=8)
    hc_ref[...] = hlast

    rec = u_ref[...] * jax.nn.gelu(gate_ref[0])
    o_ref[0] = _rms(rec, g_ref[...]).astype(o_ref.dtype)


def _lru_call(lx, gate, cw, cb, wa, ba, wx, bx, lam, g, ts):
    B, S, C = lx.shape
    const = lambda shape: pl.BlockSpec(shape, lambda b, s: (0,) * len(shape))
    tok = pl.BlockSpec((1, ts, C), lambda b, s: (b, s, 0))
    return pl.pallas_call(
        functools.partial(_lru_kernel, ts=ts),
        grid=(B, S // ts),
        in_specs=[tok, tok, const(cw.shape), const((1, C)), const(wa.shape), const((1, C)),
                  const(wx.shape), const((1, C)), const((1, C)), const((1, C))],
        out_specs=tok,
        out_shape=jax.ShapeDtypeStruct((B, S, C), BF16),
        scratch_shapes=[pltpu.VMEM((SUBLANES, C), F32), pltpu.VMEM((SUBLANES, C), F32),
                        pltpu.VMEM((ts, C), F32), pltpu.VMEM((ts, C), F32)],
        compiler_params=pltpu.CompilerParams(
            dimension_semantics=("arbitrary", "arbitrary"),
            vmem_limit_bytes=VMEM_LIMIT_BYTES),
        name="lru",
    )(lx, gate, cw, cb, wa, ba, wx, bx, lam, g)


def _mlp_kernel(x_ref, att_ref, rec_ref, ga_ref, wo_ref, g2_ref, wg_ref, wu_ref, wd_ref,
                o_ref, act_ref, *, tf):
    aw = att_ref.shape[-1]
    attn = _rms(att_ref[0].astype(F32), ga_ref[...]).astype(BF16)
    x1 = x_ref[0] + _dot(attn, wo_ref[0:aw, :]) + _dot(rec_ref[0], wo_ref[aw:, :])
    h2 = _rms(x1, g2_ref[...]).astype(BF16)
    dff = wg_ref.shape[1]
    for c in range(dff // tf):
        cols = slice(c * tf, (c + 1) * tf)
        act = jax.nn.silu(_dot(h2, wg_ref[:, cols])) * _dot(h2, wu_ref[:, cols])
        act_ref[:, cols] = act.astype(BF16)
    o_ref[0] = x1 + _dot(act_ref[...], wd_ref[...])


def _mlp_call(x, att, rec, ga, wo, g2, wg, wu, wd, tm, tf):
    B, S, D = x.shape
    aw = att.shape[-1]
    lw = rec.shape[-1]
    const = lambda shape: pl.BlockSpec(shape, lambda b, s: (0,) * len(shape))
    tok = lambda w: pl.BlockSpec((1, tm, w), lambda b, s: (b, s, 0))
    return pl.pallas_call(
        functools.partial(_mlp_kernel, tf=tf),
        grid=(B, S // tm),
        in_specs=[tok(D), tok(aw), tok(lw), const((1, aw)), const(wo.shape), const((1, D)),
                  const(wg.shape), const(wu.shape), const(wd.shape)],
        out_specs=tok(D),
        out_shape=jax.ShapeDtypeStruct((B, S, D), x.dtype),
        scratch_shapes=[pltpu.VMEM((tm, wg.shape[1]), BF16)],
        compiler_params=pltpu.CompilerParams(
            dimension_semantics=("arbitrary", "arbitrary"),
            vmem_limit_bytes=VMEM_LIMIT_BYTES),
        name="mlp",
    )(x, att, rec, ga, wo, g2, wg, wu, wd)


def _block_diag(w):
    n, d, e = w.shape
    eye = jnp.eye(n, dtype=w.dtype)
    return jnp.einsum('nde,nm->ndme', w, eye).reshape(n * d, n * e)


def _tiles(S):
    tm = min(512, S)
    tq = min(256, S)
    ts = min(512, S)
    return tm, tq, ts


def _layer(x, norm1_g, w_in, q_norm_g, k_norm_g, b_f, conv_w, conv_b, w_a, b_a, w_x, b_x,
           lam, attn_out_g, lru_out_g, w_out, norm2_g, w_gate, w_up, w_down):
    B, S, D = x.shape
    aw = ATT_WIDTH
    lw = D - aw
    tm, tq, ts = _tiles(S)
    row = lambda a: a.reshape(1, -1).astype(F32)

    wqkv = w_in[:, :3 * aw].astype(BF16)
    wf = jnp.pad(w_in[:, 3 * aw:3 * aw + ATT_HEADS], ((0, 0), (0, LANES - ATT_HEADS))).astype(BF16)
    wlru = w_in[:, 3 * aw + ATT_HEADS:].astype(BF16)
    bf = jnp.pad(b_f.astype(F32), (0, LANES - ATT_HEADS)).reshape(1, LANES)
    gq = row(jnp.tile(q_norm_g, ATT_HEADS))
    gk = row(jnp.tile(k_norm_g, ATT_HEADS))
    head_id = jnp.arange(aw) // HEAD_DIM
    hsum = (head_id[:, None] == head_id[None, :]).astype(BF16)
    tri = (jnp.arange(tm)[:, None] >= jnp.arange(tm)[None, :]).astype(BF16)

    q, k, v, ft, lx, gate = _proj_call(x, row(norm1_g), wqkv, wf, wlru, gq, gk, bf, hsum, tri, tm)
    att = _attn_call(q, k, v, ft, tq)
    rec = _lru_call(lx, gate, conv_w.astype(F32), row(conv_b), _block_diag(w_a).astype(BF16),
                    row(b_a), _block_diag(w_x).astype(BF16), row(b_x), row(lam),
                    row(lru_out_g), ts)
    tf = 256
    return _mlp_call(x, att, rec, row(attn_out_g), w_out.astype(BF16), row(norm2_g),
                     w_gate.astype(BF16), w_up.astype(BF16), w_down.astype(BF16), tm, tf)


def kernel(x, norm1_g, w_in, q_norm_g, k_norm_g, b_f, conv_w, conv_b, w_a, b_a, w_x, b_x, lam,
           attn_out_g, lru_out_g, w_out, norm2_g, w_gate, w_up, w_down):
    depth = norm1_g.shape[0]
    for l in range(depth):
        x = _layer(x, norm1_g[l], w_in[l], q_norm_g[l], k_norm_g[l], b_f[l], conv_w[l],
                   conv_b[l], w_a[l], b_a[l], w_x[l], b_x[l], lam[l], attn_out_g[l],
                   lru_out_g[l], w_out[l], norm2_g[l], w_gate[l], w_up[l], w_down[l])
    return x
```

```python
import functools
import math

import jax
import jax.numpy as jnp
from jax import lax
from jax.experimental import pallas as pl
from jax.experimental.pallas import tpu as pltpu

ATT_HEADS = 8
HEAD_DIM = 64
ATT_WIDTH = ATT_HEADS * HEAD_DIM
LRU_BLOCKS = 8
CONV_WIDTH = 4
LRU_C = 8.0
NORM_EPS = 1e-6
LOG2E = 1.4426950408889634

LANES = 128
SUBLANES = 8
HEAD_PAIR = 2 * HEAD_DIM
VMEM_LIMIT_BYTES = 56 * 1024 * 1024

F32 = jnp.float32
BF16 = jnp.bfloat16


def _dot(a, b):
    return jnp.dot(a, b, preferred_element_type=F32)


def _dot_nt(a, b):
    return lax.dot_general(a, b, (((1,), (1,)), ((), ())), preferred_element_type=F32)


def _rms(xf, g):
    return xf * lax.rsqrt(jnp.mean(xf * xf, axis=-1, keepdims=True) + NORM_EPS) * g


def _split3(x):
    hi = x.astype(BF16)
    r = x - hi.astype(F32)
    mid = r.astype(BF16)
    lo = (r - mid.astype(F32)).astype(BF16)
    return hi, mid, lo


def _proj_kernel(x_ref, g1_ref, wqkv_ref, wf_ref, wlru_ref, gq_ref, gk_ref, bf_ref,
                 hsum_ref, tri_ref, sel_ref,
                 qt_ref, kx_ref, vt_ref, lx_ref, gate_ref, carry_ref):
    si = pl.program_id(1)

    @pl.when(si == 0)
    def _():
        carry_ref[...] = jnp.zeros_like(carry_ref)

    h = _rms(x_ref[0], g1_ref[...]).astype(BF16)
    hsum = hsum_ref[...]

    def head_norm(t, g):
        ms = _dot((t * t).astype(BF16), hsum) * (1.0 / HEAD_DIM)
        return t * lax.rsqrt(ms + NORM_EPS) * g

    aw = ATT_WIDTH
    q = head_norm(_dot(h, wqkv_ref[:, 0:aw]), gq_ref[...]) * (LOG2E / math.sqrt(HEAD_DIM))
    qt_ref[0] = q.T.astype(BF16)
    k = head_norm(_dot(h, wqkv_ref[:, aw:2 * aw]), gk_ref[...]).astype(BF16)
    vt_ref[0, 0] = _dot(h, wqkv_ref[:, 2 * aw:3 * aw]).T.astype(BF16)

    lw = lx_ref.shape[-1]
    lx_ref[0] = _dot(h, wlru_ref[:, 0:lw])
    gate_ref[0] = _dot(h, wlru_ref[:, lw:2 * lw])

    logf = jax.nn.log_sigmoid(_dot(h, wf_ref[...]) + bf_ref[...])
    tri = tri_ref[...]
    hi, mid, lo = _split3(logf)
    cum = _dot(tri, hi) + _dot(tri, mid) + _dot(tri, lo) + carry_ref[0:1, :]
    tm = cum.shape[0]
    carry_ref[...] = jnp.broadcast_to(cum[tm - 1:tm, :], carry_ref.shape)

    pieces = jnp.concatenate(_split3(cum * LOG2E), axis=1)
    ext = _dot(pieces, sel_ref[...]).astype(BF16)
    for p in range(aw // HEAD_PAIR):
        kx_ref[0, :, 2 * p * HEAD_PAIR:(2 * p + 1) * HEAD_PAIR] = k[:, p * HEAD_PAIR:(p + 1) * HEAD_PAIR]
        kx_ref[0, :, (2 * p + 1) * HEAD_PAIR:(2 * p + 2) * HEAD_PAIR] = ext[:, p * HEAD_PAIR:(p + 1) * HEAD_PAIR]


def _proj_call(x, g1, wqkv, wf, wlru, gq, gk, bf, hsum, tri, sel, tm):
    B, S, D = x.shape
    aw = ATT_WIDTH
    lw = wlru.shape[1] // 2
    const = lambda shape: pl.BlockSpec(shape, lambda b, s: (0,) * len(shape))
    tok = lambda w: pl.BlockSpec((1, tm, w), lambda b, s: (b, s, 0))
    return pl.pallas_call(
        _proj_kernel,
        grid=(B, S // tm),
        in_specs=[tok(D), const((1, D)), const(wqkv.shape), const(wf.shape),
                  const(wlru.shape), const((1, aw)), const((1, aw)), const((1, LANES)),
                  const(hsum.shape), const(tri.shape), const(sel.shape)],
        out_specs=[pl.BlockSpec((1, aw, tm), lambda b, s: (b, 0, s)),
                   tok(2 * aw),
                   pl.BlockSpec((1, 1, aw, tm), lambda b, s: (b, s, 0, 0)),
                   tok(lw), tok(lw)],
        out_shape=[jax.ShapeDtypeStruct((B, aw, S), BF16),
                   jax.ShapeDtypeStruct((B, S, 2 * aw), BF16),
                   jax.ShapeDtypeStruct((B, S // tm, aw, tm), BF16),
                   jax.ShapeDtypeStruct((B, S, lw), F32),
                   jax.ShapeDtypeStruct((B, S, lw), F32)],
        scratch_shapes=[pltpu.VMEM((SUBLANES, LANES), F32)],
        compiler_params=pltpu.CompilerParams(
            dimension_semantics=("arbitrary", "arbitrary"),
            vmem_limit_bytes=VMEM_LIMIT_BYTES),
        name="proj",
    )(x, g1, wqkv, wf, wlru, gq, gk, bf, hsum, tri, sel)


def _softmax_step(state, s, vb, head):
    m, l, acc = state
    m_new = jnp.maximum(m, jnp.max(s, axis=0, keepdims=True))
    alpha = jnp.exp2(m - m_new)
    p = jnp.exp2(s - m_new)
    l = alpha * l + jnp.sum(p, axis=0, keepdims=True)
    pv = _dot(vb, p.astype(BF16))
    acc = alpha * acc + pv[head * HEAD_DIM:(head + 1) * HEAD_DIM]
    return m_new, l, acc


def _attn_kernel(qt_ref, kx_ref, vt_ref, o_ref, *, tq, tk, td):
    i = pl.program_id(2)
    qt = qt_ref[0]
    zeros = jnp.zeros((HEAD_DIM, tq), BF16)
    erow = lax.broadcasted_iota(jnp.int32, (HEAD_PAIR, tq), 0)
    bias_rows = lambda h: jnp.where((erow >= 3 * h) & (erow < 3 * h + 3), -1.0, 0.0).astype(BF16)
    qs = (jnp.concatenate([qt[0:HEAD_DIM], zeros, bias_rows(0)], axis=0),
          jnp.concatenate([zeros, qt[HEAD_DIM:HEAD_PAIR], bias_rows(1)], axis=0))
    qst = jnp.concatenate(qs, axis=1)

    def full_step(j, carry):
        kb = kx_ref[0, pl.ds(pl.multiple_of(j * tk, tk), tk), :]
        vb = vt_ref[0, j, 0]
        s2 = _dot(kb, qst)
        return tuple(_softmax_step(carry[h], s2[:, h * tq:(h + 1) * tq], vb, h) for h in range(2))

    init = tuple((jnp.full((1, tq), -jnp.inf, F32), jnp.zeros((1, tq), F32),
                  jnp.zeros((HEAD_DIM, tq), F32)) for _ in range(2))
    carry = lax.fori_loop(0, i * (tq // tk), full_step, init)

    krow = lax.broadcasted_iota(jnp.int32, (td, td), 0)
    qcol = lax.broadcasted_iota(jnp.int32, (td, td), 1)
    keep = krow <= qcol
    for c in range(tq // td):
        c0 = c * td
        w = tq - c0
        kb = kx_ref[0, pl.ds(pl.multiple_of(i * tq + c0, td), td), :]
        lo = c0 % tk
        vb = vt_ref[0, i * (tq // tk) + c0 // tk, 0, :, lo:lo + td]
        s2 = _dot(kb, jnp.concatenate([qs[0][:, c0:], qs[1][:, c0:]], axis=1))
        new = []
        for h in range(2):
            m, l, acc = carry[h]
            s = s2[:, h * w:(h + 1) * w]
            sd = jnp.where(keep, s[:, 0:td], -jnp.inf)
            s = sd if w == td else jnp.concatenate([sd, s[:, td:]], axis=1)
            part = _softmax_step((m[:, c0:], l[:, c0:], acc[:, c0:]), s, vb, h)
            if c0:
                part = tuple(jnp.concatenate([old[:, 0:c0], upd], axis=1)
                             for old, upd in zip((m, l, acc), part))
            new.append(part)
        carry = tuple(new)

    (_, la, acca), (_, lb, accb) = carry
    ot = jnp.concatenate([acca / la, accb / lb], axis=0)
    o_ref[0] = ot.T.astype(o_ref.dtype)


def _attn_call(qt, kx, vt, tq, tk, td):
    B, aw, S = qt.shape
    npair = aw // HEAD_PAIR
    vt5 = vt.reshape(B, S // tk, npair, HEAD_PAIR, tk)
    return pl.pallas_call(
        functools.partial(_attn_kernel, tq=tq, tk=tk, td=td),
        grid=(B, npair, S // tq),
        in_specs=[pl.BlockSpec((1, HEAD_PAIR, tq), lambda b, p, i: (b, p, i)),
                  pl.BlockSpec((1, S, 2 * HEAD_PAIR), lambda b, p, i: (b, 0, p)),
                  pl.BlockSpec((1, S // tk, 1, HEAD_PAIR, tk), lambda b, p, i: (b, 0, p, 0, 0))],
        out_specs=pl.BlockSpec((1, tq, HEAD_PAIR), lambda b, p, i: (b, i, p)),
        out_shape=jax.ShapeDtypeStruct((B, S, aw), BF16),
        compiler_params=pltpu.CompilerParams(
            dimension_semantics=("arbitrary", "arbitrary", "arbitrary"),
            vmem_limit_bytes=VMEM_LIMIT_BYTES),
        name="attn",
    )(qt, kx, vt5)


def _lru_kernel(lx_ref, gate_ref, cw_ref, cb_ref, wa_ref, ba_ref, wx_ref, bx_ref,
                lam_ref, g_ref, o_ref, halo_ref, hc_ref, a_ref, u_ref, *, ts):
    si = pl.program_id(1)

    @pl.when(si == 0)
    def _():
        halo_ref[...] = jnp.zeros_like(halo_ref)
        hc_ref[...] = jnp.zeros_like(hc_ref)

    lx = lx_ref[0]
    ext = jnp.concatenate([halo_ref[...], lx], axis=0)
    halo_ref[...] = lx[ts - SUBLANES:ts, :]
    xc = cb_ref[...]
    for j in range(CONV_WIDTH):
        off = SUBLANES - (CONV_WIDTH - 1) + j
        xc = xc + ext[off:off + ts, :] * cw_ref[j:j + 1, :]

    xcb = xc.astype(BF16)
    r = jax.nn.sigmoid(_dot(xcb, wa_ref[...]) + ba_ref[...])
    gi = jax.nn.sigmoid(_dot(xcb, wx_ref[...]) + bx_ref[...])
    log_a = -LRU_C * r * jax.nn.softplus(-lam_ref[...])
    a = jnp.exp(log_a)
    u = jnp.sqrt(-jnp.tanh(log_a) * (1.0 + a * a)) * (gi * xc)

    rin = lax.broadcasted_iota(jnp.int32, a.shape, 0) % SUBLANES
    for sh in (1, 2, 4):
        ok = rin >= sh
        a_prev = jnp.where(ok, pltpu.roll(a, sh, axis=0), 1.0)
        u_prev = jnp.where(ok, pltpu.roll(u, sh, axis=0), 0.0)
        u = a * u_prev + u
        a = a * a_prev
    a_ref[...] = a
    u_ref[...] = u

    def group(g, hprev):
        rows = pl.ds(pl.multiple_of(g * SUBLANES, SUBLANES), SUBLANES)
        hg = a_ref[rows, :] * hprev + u_ref[rows, :]
        u_ref[rows, :] = hg
        return jnp.broadcast_to(hg[SUBLANES - 1:SUBLANES, :], hg.shape)

    hlast = lax.fori_loop(0, ts // SUBLANES, group, hc_ref[...], unroll=8)
    hc_ref[...] = hlast

    rec = u_ref[...] * jax.nn.gelu(gate_ref[0])
    o_ref[0] = _rms(rec, g_ref[...]).astype(o_ref.dtype)


def _lru_call(lx, gate, cw, cb, wa, ba, wx, bx, lam, g, ts):
    B, S, C = lx.shape
    const = lambda shape: pl.BlockSpec(shape, lambda b, s: (0,) * len(shape))
    tok = pl.BlockSpec((1, ts, C), lambda b, s: (b, s, 0))
    return pl.pallas_call(
        functools.partial(_lru_kernel, ts=ts),
        grid=(B, S // ts),
        in_specs=[tok, tok, const(cw.shape), const((1, C)), const(wa.shape), const((1, C)),
                  const(wx.shape), const((1, C)), const((1, C)), const((1, C))],
        out_specs=tok,
        out_shape=jax.ShapeDtypeStruct((B, S, C), BF16),
        scratch_shapes=[pltpu.VMEM((SUBLANES, C), F32), pltpu.VMEM((SUBLANES, C), F32),
                        pltpu.VMEM((ts, C), F32), pltpu.VMEM((ts, C), F32)],
        compiler_params=pltpu.CompilerParams(
            dimension_semantics=("arbitrary", "arbitrary"),
            vmem_limit_bytes=VMEM_LIMIT_BYTES),
        name="lru",
    )(lx, gate, cw, cb, wa, ba, wx, bx, lam, g)


def _mlp_kernel(x_ref, att_ref, rec_ref, ga_ref, wo_ref, g2_ref, wg_ref, wu_ref, wd_ref,
                o_ref, act_ref, *, tf):
    aw = att_ref.shape[-1]
    attn = _rms(att_ref[0].astype(F32), ga_ref[...]).astype(BF16)
    x1 = x_ref[0] + _dot(attn, wo_ref[0:aw, :]) + _dot(rec_ref[0], wo_ref[aw:, :])
    h2 = _rms(x1, g2_ref[...]).astype(BF16)
    dff = wg_ref.shape[1]
    for c in range(dff // tf):
        cols = slice(c * tf, (c + 1) * tf)
        act = jax.nn.silu(_dot(h2, wg_ref[:, cols])) * _dot(h2, wu_ref[:, cols])
        act_ref[:, cols] = act.astype(BF16)
    o_ref[0] = x1 + _dot(act_ref[...], wd_ref[...])


def _mlp_call(x, att, rec, ga, wo, g2, wg, wu, wd, tm, tf):
    B, S, D = x.shape
    aw = att.shape[-1]
    lw = rec.shape[-1]
    const = lambda shape: pl.BlockSpec(shape, lambda b, s: (0,) * len(shape))
    tok = lambda w: pl.BlockSpec((1, tm, w), lambda b, s: (b, s, 0))
    return pl.pallas_call(
        functools.partial(_mlp_kernel, tf=tf),
        grid=(B, S // tm),
        in_specs=[tok(D), tok(aw), tok(lw), const((1, aw)), const(wo.shape), const((1, D)),
                  const(wg.shape), const(wu.shape), const(wd.shape)],
        out_specs=tok(D),
        out_shape=jax.ShapeDtypeStruct((B, S, D), x.dtype),
        scratch_shapes=[pltpu.VMEM((tm, wg.shape[1]), BF16)],
        compiler_params=pltpu.CompilerParams(
            dimension_semantics=("arbitrary", "arbitrary"),
            vmem_limit_bytes=VMEM_LIMIT_BYTES),
        name="mlp",
    )(x, att, rec, ga, wo, g2, wg, wu, wd)


def _block_diag(w):
    n, d, e = w.shape
    eye = jnp.eye(n, dtype=w.dtype)
    return jnp.einsum('nde,nm->ndme', w, eye).reshape(n * d, n * e)


def _bias_select():
    npair = ATT_WIDTH // HEAD_PAIR
    h = jnp.arange(ATT_HEADS)
    sel = jnp.zeros((3 * LANES, npair * HEAD_PAIR), F32)
    for piece in range(3):
        sel = sel.at[piece * LANES + h, (h // 2) * HEAD_PAIR + 3 * (h % 2) + piece].set(1.0)
    return sel.astype(BF16)


def _tiles(S):
    tm = min(512, S)
    tq = min(1024, S)
    td = min(256, S)
    ts = min(512, S)
    return tm, tq, td, ts


def _layer(x, norm1_g, w_in, q_norm_g, k_norm_g, b_f, conv_w, conv_b, w_a, b_a, w_x, b_x,
           lam, attn_out_g, lru_out_g, w_out, norm2_g, w_gate, w_up, w_down):
    B, S, D = x.shape
    aw = ATT_WIDTH
    lw = D - aw
    tm, tq, td, ts = _tiles(S)
    row = lambda a: a.reshape(1, -1).astype(F32)

    wqkv = w_in[:, :3 * aw].astype(BF16)
    wf = jnp.pad(w_in[:, 3 * aw:3 * aw + ATT_HEADS], ((0, 0), (0, LANES - ATT_HEADS))).astype(BF16)
    wlru = w_in[:, 3 * aw + ATT_HEADS:].astype(BF16)
    bf = jnp.pad(b_f.astype(F32), (0, LANES - ATT_HEADS)).reshape(1, LANES)
    gq = row(jnp.tile(q_norm_g, ATT_HEADS))
    gk = row(jnp.tile(k_norm_g, ATT_HEADS))
    head_id = jnp.arange(aw) // HEAD_DIM
    hsum = (head_id[:, None] == head_id[None, :]).astype(BF16)
    tri = (jnp.arange(tm)[:, None] >= jnp.arange(tm)[None, :]).astype(BF16)

    qt, kx, vt, lx, gate = _proj_call(x, row(norm1_g), wqkv, wf, wlru, gq, gk, bf, hsum, tri,
                                      _bias_select(), tm)
    att = _attn_call(qt, kx, vt, tq, tm, td)
    rec = _lru_call(lx, gate, conv_w.astype(F32), row(conv_b), _block_diag(w_a).astype(BF16),
                    row(b_a), _block_diag(w_x).astype(BF16), row(b_x), row(lam),
                    row(lru_out_g), ts)
    tf = 256
    return _mlp_call(x, att, rec, row(attn_out_g), w_out.astype(BF16), row(norm2_g),
                     w_gate.astype(BF16), w_up.astype(BF16), w_down.astype(BF16), tm, tf)


def kernel(x, norm1_g, w_in, q_norm_g, k_norm_g, b_f, conv_w, conv_b, w_a, b_a, w_x, b_x, lam,
           attn_out_g, lru_out_g, w_out, norm2_g, w_gate, w_up, w_down):
    depth = norm1_g.shape[0]
    for l in range(depth):
        x = _layer(x, norm1_g[l], w_in[l], q_norm_g[l], k_norm_g[l], b_f[l], conv_w[l],
                   conv_b[l], w_a[l], b_a[l], w_x[l], b_x[l], lam[l], attn_out_g[l],
                   lru_out_g[l], w_out[l], norm2_g[l], w_gate[l], w_up[l], w_down[l])
    return x
```

```python
import functools
import math

import jax
import jax.numpy as jnp
from jax import lax
from jax.experimental import pallas as pl
from jax.experimental.pallas import tpu as pltpu

ATT_HEADS = 8
HEAD_DIM = 64
ATT_WIDTH = ATT_HEADS * HEAD_DIM
CONV_WIDTH = 4
LRU_C = 8.0
NORM_EPS = 1e-6
LOG2E = 1.4426950408889634

LANES = 128
SUBLANES = 8
HEAD_PAIR = 2 * HEAD_DIM
BIAS_ROWS = 16
DIRECT_SOFTMAX_MAX_BITS = 100.0
VMEM_LIMIT_BYTES = 56 * 1024 * 1024

F32 = jnp.float32
BF16 = jnp.bfloat16


def _dot(a, b):
    return jnp.dot(a, b, preferred_element_type=F32)


def _dot_nt(a, b):
    return lax.dot_general(a, b, (((1,), (1,)), ((), ())), preferred_element_type=F32)


def _rms(xf, g):
    return xf * lax.rsqrt(jnp.mean(xf * xf, axis=-1, keepdims=True) + NORM_EPS) * g


def _split3(x):
    hi = x.astype(BF16)
    r = x - hi.astype(F32)
    mid = r.astype(BF16)
    lo = (r - mid.astype(F32)).astype(BF16)
    return hi, mid, lo


def _proj_kernel(x_ref, g1_ref, wqkv_ref, wf_ref, wlru_ref, gq_ref, gk_ref, bf_ref,
                 hsum_ref, tri_ref, selk_ref, selq_ref, kone_ref,
                 qt_ref, qe_ref, kx_ref, vt_ref, lx_ref, gate_ref, carry_ref):
    si = pl.program_id(1)

    @pl.when(si == 0)
    def _():
        carry_ref[...] = jnp.zeros_like(carry_ref)

    h = _rms(x_ref[0], g1_ref[...]).astype(BF16)
    hsum = hsum_ref[...]

    def head_norm(t, g):
        ms = _dot((t * t).astype(BF16), hsum) * (1.0 / HEAD_DIM)
        return t * lax.rsqrt(ms + NORM_EPS) * g

    aw = ATT_WIDTH
    q = head_norm(_dot(h, wqkv_ref[:, 0:aw]), gq_ref[...]) * (LOG2E / math.sqrt(HEAD_DIM))
    qt_ref[0] = q.T.astype(BF16)
    k = head_norm(_dot(h, wqkv_ref[:, aw:2 * aw]), gk_ref[...]).astype(BF16)
    vt_ref[0, 0] = _dot(h, wqkv_ref[:, 2 * aw:3 * aw]).T.astype(BF16)

    lw = lx_ref.shape[-1]
    lx_ref[0] = _dot(h, wlru_ref[:, 0:lw])
    gate_ref[0] = _dot(h, wlru_ref[:, lw:2 * lw])

    logf = jax.nn.log_sigmoid(_dot(h, wf_ref[...]) + bf_ref[...])
    tri = tri_ref[...]
    hi, mid, lo = _split3(logf)
    cum = _dot(tri, hi) + _dot(tri, mid) + _dot(tri, lo) + carry_ref[0:1, :]
    tm = cum.shape[0]
    carry_ref[...] = jnp.broadcast_to(cum[tm - 1:tm, :], carry_ref.shape)

    pieces = jnp.concatenate(_split3(cum * LOG2E), axis=1)
    ext_k = (_dot(pieces, selk_ref[...]) + kone_ref[...]).astype(BF16)
    for p in range(aw // HEAD_PAIR):
        pair = slice(p * HEAD_PAIR, (p + 1) * HEAD_PAIR)
        kx_ref[0, :, 2 * p * HEAD_PAIR:(2 * p + 1) * HEAD_PAIR] = k[:, pair]
        kx_ref[0, :, (2 * p + 1) * HEAD_PAIR:(2 * p + 2) * HEAD_PAIR] = ext_k[:, pair]
    qe_ref[0] = _dot_nt(selq_ref[...], pieces).astype(BF16)


def _proj_call(x, g1, wqkv, wf, wlru, gq, gk, bf, hsum, tri, sel_k, sel_q, kone, tm):
    B, S, D = x.shape
    aw = ATT_WIDTH
    npair = aw // HEAD_PAIR
    lw = wlru.shape[1] // 2
    const = lambda shape: pl.BlockSpec(shape, lambda b, s: (0,) * len(shape))
    tok = lambda w: pl.BlockSpec((1, tm, w), lambda b, s: (b, s, 0))
    return pl.pallas_call(
        _proj_kernel,
        grid=(B, S // tm),
        in_specs=[tok(D), const((1, D)), const(wqkv.shape), const(wf.shape),
                  const(wlru.shape), const((1, aw)), const((1, aw)), const((1, LANES)),
                  const(hsum.shape), const(tri.shape), const(sel_k.shape), const(sel_q.shape),
                  const((1, aw))],
        out_specs=[pl.BlockSpec((1, aw, tm), lambda b, s: (b, 0, s)),
                   pl.BlockSpec((1, npair * BIAS_ROWS, tm), lambda b, s: (b, 0, s)),
                   tok(2 * aw),
                   pl.BlockSpec((1, 1, aw, tm), lambda b, s: (b, s, 0, 0)),
                   tok(lw), tok(lw)],
        out_shape=[jax.ShapeDtypeStruct((B, aw, S), BF16),
                   jax.ShapeDtypeStruct((B, npair * BIAS_ROWS, S), BF16),
                   jax.ShapeDtypeStruct((B, S, 2 * aw), BF16),
                   jax.ShapeDtypeStruct((B, S // tm, aw, tm), BF16),
                   jax.ShapeDtypeStruct((B, S, lw), F32),
                   jax.ShapeDtypeStruct((B, S, lw), F32)],
        scratch_shapes=[pltpu.VMEM((SUBLANES, LANES), F32)],
        compiler_params=pltpu.CompilerParams(
            dimension_semantics=("arbitrary", "arbitrary"),
            vmem_limit_bytes=VMEM_LIMIT_BYTES),
        name="proj",
    )(x, g1, wqkv, wf, wlru, gq, gk, bf, hsum, tri, sel_k, sel_q, kone)


def _stacked_queries(qt, qe, direct):
    tq = qt.shape[1]
    zeros = jnp.zeros((HEAD_DIM, tq), BF16)
    pad = jnp.zeros((HEAD_PAIR - BIAS_ROWS, tq), BF16)
    erow = lax.broadcasted_iota(jnp.int32, (BIAS_ROWS, tq), 0)

    def bias_rows(h):
        rows = jnp.where((erow >= 3 * h) & (erow < 3 * h + 3), -1.0, 0.0).astype(BF16)
        if direct:
            rows = jnp.where((erow >= 6 + 3 * h) & (erow < 9 + 3 * h), qe, rows)
        return rows

    return (jnp.concatenate([qt[0:HEAD_DIM], zeros, bias_rows(0), pad], axis=0),
            jnp.concatenate([zeros, qt[HEAD_DIM:HEAD_PAIR], bias_rows(1), pad], axis=0))


def _softmax_step(state, s, vb, head, direct):
    m, l, acc = state
    rows = slice(head * HEAD_DIM, (head + 1) * HEAD_DIM)
    if direct:
        p = jnp.exp2(s)
        l = l + jnp.sum(p, axis=0, keepdims=True)
        return None, l, acc + _dot(vb, p.astype(BF16))[rows]
    m_new = jnp.maximum(m, jnp.max(s, axis=0, keepdims=True))
    alpha = jnp.exp2(m - m_new)
    p = jnp.exp2(s - m_new)
    l = alpha * l + jnp.sum(p, axis=0, keepdims=True)
    return m_new, l, alpha * acc + _dot(vb, p.astype(BF16))[rows]


def _attn_kernel(qt_ref, qe_ref, kx_ref, vt_ref, o_ref, *, tq, tk, td, direct):
    i = pl.program_id(2)
    qs = _stacked_queries(qt_ref[0], qe_ref[0], direct)
    qst = jnp.concatenate(qs, axis=1)
    nk = tq // tk

    def full_step(j, carry):
        kb = kx_ref[0, pl.ds(pl.multiple_of(j * tq, tq), tq), :]
        vb = jnp.concatenate([vt_ref[0, j * nk + n, 0] for n in range(nk)], axis=1)
        s2 = _dot(kb, qst)
        return tuple(_softmax_step(carry[h], s2[:, h * tq:(h + 1) * tq], vb, h, direct)
                     for h in range(2))

    init = tuple((None if direct else jnp.full((1, tq), -jnp.inf, F32), jnp.zeros((1, tq), F32),
                  jnp.zeros((HEAD_DIM, tq), F32)) for _ in range(2))
    carry = lax.fori_loop(0, i, full_step, init)

    krow = lax.broadcasted_iota(jnp.int32, (td, td), 0)
    qcol = lax.broadcasted_iota(jnp.int32, (td, td), 1)
    keep = krow <= qcol
    for c in range(tq // td):
        c0 = c * td
        w = tq - c0
        kb = kx_ref[0, pl.ds(pl.multiple_of(i * tq + c0, td), td), :]
        lo = c0 % tk
        vb = vt_ref[0, i * nk + c0 // tk, 0, :, lo:lo + td]
        s2 = _dot(kb, jnp.concatenate([qs[0][:, c0:], qs[1][:, c0:]], axis=1))
        new = []
        for h in range(2):
            s = s2[:, h * w:(h + 1) * w]
            sd = jnp.where(keep, s[:, 0:td], -jnp.inf)
            s = sd if w == td else jnp.concatenate([sd, s[:, td:]], axis=1)
            old = carry[h]
            part = _softmax_step(tuple(None if t is None else t[:, c0:] for t in old), s, vb, h,
                                 direct)
            if c0:
                part = tuple(None if t is None else jnp.concatenate([t[:, 0:c0], u], axis=1)
                             for t, u in zip(old, part))
            new.append(part)
        carry = tuple(new)

    (_, la, acca), (_, lb, accb) = carry
    ot = jnp.concatenate([acca / la, accb / lb], axis=0)
    o_ref[0] = ot.T.astype(o_ref.dtype)


def _attn_call(qt, qe, kx, vt, tq, tk, td, direct):
    B, aw, S = qt.shape
    npair = aw // HEAD_PAIR
    vt5 = vt.reshape(B, S // tk, npair, HEAD_PAIR, tk)
    return pl.pallas_call(
        functools.partial(_attn_kernel, tq=tq, tk=tk, td=td, direct=direct),
        grid=(B, npair, S // tq),
        in_specs=[pl.BlockSpec((1, HEAD_PAIR, tq), lambda b, p, i: (b, p, i)),
                  pl.BlockSpec((1, BIAS_ROWS, tq), lambda b, p, i: (b, p, i)),
                  pl.BlockSpec((1, S, 2 * HEAD_PAIR), lambda b, p, i: (b, 0, p)),
                  pl.BlockSpec((1, S // tk, 1, HEAD_PAIR, tk), lambda b, p, i: (b, 0, p, 0, 0))],
        out_specs=pl.BlockSpec((1, tq, HEAD_PAIR), lambda b, p, i: (b, i, p)),
        out_shape=jax.ShapeDtypeStruct((B, S, aw), BF16),
        compiler_params=pltpu.CompilerParams(
            dimension_semantics=("arbitrary", "arbitrary", "arbitrary"),
            vmem_limit_bytes=VMEM_LIMIT_BYTES),
        name="attn_direct" if direct else "attn_online",
    )(qt, qe, kx, vt5)


def _direct_softmax_is_safe(norm1_g, w_v, q_norm_g, k_norm_g):
    d_model = norm1_g.shape[0]
    qk_bits = LOG2E * math.sqrt(HEAD_DIM) * jnp.max(jnp.abs(q_norm_g)) * jnp.max(jnp.abs(k_norm_g))
    v_bound = (math.sqrt(d_model) * jnp.max(jnp.abs(norm1_g))
               * jnp.max(jnp.sqrt(jnp.sum(jnp.square(w_v.astype(F32)), axis=0))))
    return qk_bits + jnp.log2(jnp.maximum(v_bound, 1.0)) < DIRECT_SOFTMAX_MAX_BITS


def _lru_gates(lx_ref, cw_ref, cb_ref, wa_ref, ba_ref, wx_ref, bx_ref, ext_ref, xc_ref, ga_ref,
               gx_ref, first):
    ts = lx_ref.shape[1]
    halo = jnp.where(first, 0.0, ext_ref[ts:ts + SUBLANES, :])
    ext_ref[0:SUBLANES, :] = halo
    ext_ref[SUBLANES:, :] = lx_ref[0]
    xc = cb_ref[...]
    for j in range(CONV_WIDTH):
        xc = xc + ext_ref[pl.ds(SUBLANES - (CONV_WIDTH - 1) + j, ts), :] * cw_ref[j:j + 1, :]
    xc_ref[...] = xc
    xcb = xc.astype(BF16)
    ga_ref[...] = _dot(xcb, wa_ref[...]) + ba_ref[...]
    gx_ref[...] = _dot(xcb, wx_ref[...]) + bx_ref[...]


def _zero_after(v):
    bits = v[0:SUBLANES, 0:LANES].astype(jnp.int32)
    return lax.shift_right_logical(lax.shift_right_logical(bits, 16), 16).astype(F32)


def _lru_piece(rows, hprev, xc_ref, ga_ref, gx_ref, gate_ref, lam_ref, g_ref, rec_ref, after):
    xc = xc_ref[rows, :]
    pre_a = ga_ref[rows, :]
    if after is not None:
        pr, C = pre_a.shape
        pre_a = pre_a + jnp.tile(_zero_after(after), (pr // SUBLANES, C // LANES))
    r = jax.nn.sigmoid(pre_a)
    gi = jax.nn.sigmoid(gx_ref[rows, :])
    log_a = -LRU_C * r * jax.nn.softplus(-lam_ref[...])
    a = jnp.exp(log_a)
    u = jnp.sqrt(-jnp.tanh(log_a) * (1.0 + a * a)) * (gi * xc)

    pr, C = xc.shape
    ng = pr // SUBLANES
    a = a.reshape(ng, SUBLANES, C)
    u = u.reshape(ng, SUBLANES, C)
    rin = lax.broadcasted_iota(jnp.int32, a.shape, 1)
    for sh in (1, 2, 4):
        ok = rin >= sh
        a_prev = jnp.where(ok, pltpu.roll(a, sh, axis=1), 1.0)
        u_prev = jnp.where(ok, pltpu.roll(u, sh, axis=1), 0.0)
        u = a * u_prev + u
        a = a * a_prev

    hs = []
    for gidx in range(ng):
        hg = a[gidx] * hprev + u[gidx]
        hs.append(hg)
        hprev = jnp.broadcast_to(hg[SUBLANES - 1:SUBLANES, :], hg.shape)
    rec = jnp.concatenate(hs, axis=0) * jax.nn.gelu(gate_ref[0, rows, :])
    rec_ref[rows, :] = _rms(rec, g_ref[...]).astype(BF16)
    return hprev


def _mix_kernel(x_ref, att_ref, lx_ref, gate_ref, cw_ref, cb_ref, wa_ref, ba_ref, wx_ref, bx_ref,
                lam_ref, gl_ref, ga_ref, wo_ref, g2_ref, wg_ref, wu_ref, wd_ref,
                o_ref, act_ref, rec_ref, ext_ref, hc_ref, xc_ref, pa_ref, px_ref,
                *, tf, pr, tiles_per_seq, n_tiles):
    g = pl.program_id(0)

    @pl.when(g == 0)
    def _():
        rec_ref[...] = jnp.zeros_like(rec_ref)
        ext_ref[...] = jnp.zeros_like(ext_ref)
        hc_ref[...] = jnp.zeros_like(hc_ref)

    aw = att_ref.shape[-1]
    attn = _rms(att_ref[0].astype(F32), ga_ref[...]).astype(BF16)
    x1 = x_ref[0] + _dot(attn, wo_ref[0:aw, :]) + _dot(rec_ref[...], wo_ref[aw:, :])
    h2 = _rms(x1, g2_ref[...]).astype(BF16)

    first = (jnp.minimum(g, n_tiles - 1) % tiles_per_seq) == 0
    _lru_gates(lx_ref, cw_ref, cb_ref, wa_ref, ba_ref, wx_ref, bx_ref, ext_ref, xc_ref, pa_ref,
               px_ref, first)
    hprev = jnp.where(first, 0.0, hc_ref[...])

    tm = x_ref.shape[1]
    dff = wg_ref.shape[1]
    prev_act = None
    nchunk, npiece = dff // tf, tm // pr
    for c in range(nchunk):
        cols = slice(c * tf, (c + 1) * tf)
        act = jax.nn.silu(_dot(h2, wg_ref[:, cols])) * _dot(h2, wu_ref[:, cols])
        act_ref[:, cols] = act.astype(BF16)
        for p in range(npiece):
            if p * nchunk // npiece == c:
                hprev = _lru_piece(slice(p * pr, (p + 1) * pr), hprev, xc_ref, pa_ref, px_ref,
                                   gate_ref, lam_ref, gl_ref, rec_ref, prev_act)
        prev_act = act
    hc_ref[...] = hprev
    o_ref[0] = x1 + _dot(act_ref[...], wd_ref[...])


def _mix_call(x, att, lx, gate, cw, cb, wa, ba, wx, bx, lam, gl, ga, wo, g2, wg, wu, wd, tm, tf,
              pr):
    B, S, D = x.shape
    aw = att.shape[-1]
    C = lx.shape[-1]
    nS = S // tm
    N = B * nS
    const = lambda shape: pl.BlockSpec(shape, lambda g: (0,) * len(shape),
                                       pipeline_mode=pl.Buffered(1))

    def cur(w):
        return pl.BlockSpec((1, tm, w), lambda g: (jnp.minimum(g, N - 1) // nS,
                                                   jnp.minimum(g, N - 1) % nS, 0))

    def prev(w):
        return pl.BlockSpec((1, tm, w), lambda g: (jnp.maximum(g - 1, 0) // nS,
                                                   jnp.maximum(g - 1, 0) % nS, 0))

    return pl.pallas_call(
        functools.partial(_mix_kernel, tf=tf, pr=pr, tiles_per_seq=nS, n_tiles=N),
        grid=(N + 1,),
        in_specs=[prev(D), prev(aw), cur(C), cur(C), const(cw.shape), const((1, C)),
                  const(wa.shape), const((1, C)), const(wx.shape), const((1, C)), const((1, C)),
                  const((1, C)), const((1, aw)), const(wo.shape), const((1, D)),
                  const(wg.shape), const(wu.shape), const(wd.shape)],
        out_specs=prev(D),
        out_shape=jax.ShapeDtypeStruct((B, S, D), x.dtype),
        scratch_shapes=[pltpu.VMEM((tm, wg.shape[1]), BF16), pltpu.VMEM((tm, C), BF16),
                        pltpu.VMEM((tm + SUBLANES, C), F32), pltpu.VMEM((SUBLANES, C), F32),
                        pltpu.VMEM((tm, C), F32), pltpu.VMEM((tm, C), F32),
                        pltpu.VMEM((tm, C), F32)],
        compiler_params=pltpu.CompilerParams(
            dimension_semantics=("arbitrary",),
            vmem_limit_bytes=VMEM_LIMIT_BYTES),
        name="mix",
    )(x, att, lx, gate, cw, cb, wa, ba, wx, bx, lam, gl, ga, wo, g2, wg, wu, wd)


def _block_diag(w):
    n, d, e = w.shape
    eye = jnp.eye(n, dtype=w.dtype)
    return jnp.einsum('nde,nm->ndme', w, eye).reshape(n * d, n * e)


def _bias_select():
    npair = ATT_WIDTH // HEAD_PAIR
    src = jnp.arange(3 * LANES)
    piece, h = src // LANES, src % LANES
    slot = 3 * (h % 2) + piece
    valid = h < ATT_HEADS
    sel_k = ((jnp.arange(ATT_WIDTH)[None, :] == ((h // 2) * HEAD_PAIR + slot)[:, None])
             & valid[:, None])
    sel_q = ((jnp.arange(npair * BIAS_ROWS)[:, None] == ((h // 2) * BIAS_ROWS + 6 + slot)[None, :])
             & valid[None, :])
    in_pair = jnp.arange(ATT_WIDTH) % HEAD_PAIR
    kone = ((in_pair >= 6) & (in_pair < 12)).astype(F32).reshape(1, ATT_WIDTH)
    return sel_k.astype(BF16), sel_q.astype(BF16), kone


def _tiles(S):
    tm = min(512, S)
    tq = min(1024, S)
    td = min(256, S)
    tf = 256
    pr = min(16, S)
    return tm, tq, td, tf, pr


def _layer(x, norm1_g, w_in, q_norm_g, k_norm_g, b_f, conv_w, conv_b, w_a, b_a, w_x, b_x,
           lam, attn_out_g, lru_out_g, w_out, norm2_g, w_gate, w_up, w_down):
    B, S, D = x.shape
    aw = ATT_WIDTH
    tm, tq, td, tf, pr = _tiles(S)
    row = lambda a: a.reshape(1, -1).astype(F32)

    wqkv = w_in[:, :3 * aw].astype(BF16)
    wf = jnp.pad(w_in[:, 3 * aw:3 * aw + ATT_HEADS], ((0, 0), (0, LANES - ATT_HEADS))).astype(BF16)
    wlru = w_in[:, 3 * aw + ATT_HEADS:].astype(BF16)
    bf = jnp.pad(b_f.astype(F32), (0, LANES - ATT_HEADS)).reshape(1, LANES)
    gq = row(jnp.tile(q_norm_g, ATT_HEADS))
    gk = row(jnp.tile(k_norm_g, ATT_HEADS))
    head_id = jnp.arange(aw) // HEAD_DIM
    hsum = (head_id[:, None] == head_id[None, :]).astype(BF16)
    tri = (jnp.arange(tm)[:, None] >= jnp.arange(tm)[None, :]).astype(BF16)

    sel_k, sel_q, kone = _bias_select()
    qt, qe, kx, vt, lx, gate = _proj_call(x, row(norm1_g), wqkv, wf, wlru, gq, gk, bf, hsum, tri,
                                          sel_k, sel_q, kone, tm)
    att = lax.cond(
        _direct_softmax_is_safe(norm1_g, w_in[:, 2 * aw:3 * aw], q_norm_g, k_norm_g),
        functools.partial(_attn_call, tq=tq, tk=tm, td=td, direct=True),
        functools.partial(_attn_call, tq=tq, tk=tm, td=td, direct=False),
        qt, qe, kx, vt)
    return _mix_call(x, att, lx, gate, conv_w.astype(F32), row(conv_b),
                     _block_diag(w_a).astype(BF16), row(b_a), _block_diag(w_x).astype(BF16),
                     row(b_x), row(lam), row(lru_out_g), row(attn_out_g), w_out.astype(BF16),
                     row(norm2_g), w_gate.astype(BF16), w_up.astype(BF16), w_down.astype(BF16),
                     tm, tf, pr)


def kernel(x, norm1_g, w_in, q_norm_g, k_norm_g, b_f, conv_w, conv_b, w_a, b_a, w_x, b_x, lam,
           attn_out_g, lru_out_g, w_out, norm2_g, w_gate, w_up, w_down):
    depth = norm1_g.shape[0]
    for l in range(depth):
        x = _layer(x, norm1_g[l], w_in[l], q_norm_g[l], k_norm_g[l], b_f[l], conv_w[l],
                   conv_b[l], w_a[l], b_a[l], w_x[l], b_x[l], lam[l], attn_out_g[l],
                   lru_out_g[l], w_out[l], norm2_g[l], w_gate[l], w_up[l], w_down[l])
    return x
```

```python
import functools
import math

import jax
import jax.numpy as jnp
from jax import lax
from jax.experimental import pallas as pl
from jax.experimental.pallas import tpu as pltpu

ATT_HEADS = 8
HEAD_DIM = 64
ATT_WIDTH = ATT_HEADS * HEAD_DIM
CONV_WIDTH = 4
LRU_C = 8.0
NORM_EPS = 1e-6
LOG2E = 1.4426950408889634

LANES = 128
SUBLANES = 8
HEAD_PAIR = 2 * HEAD_DIM
BIAS_ROWS = 16
DIRECT_SOFTMAX_MAX_BITS = 100.0
VMEM_LIMIT_BYTES = 56 * 1024 * 1024

F32 = jnp.float32
BF16 = jnp.bfloat16


def _dot(a, b):
    return jnp.dot(a, b, preferred_element_type=F32)


def _dot_nt(a, b):
    return lax.dot_general(a, b, (((1,), (1,)), ((), ())), preferred_element_type=F32)


def _rms(xf, g):
    return xf * lax.rsqrt(jnp.mean(xf * xf, axis=-1, keepdims=True) + NORM_EPS) * g


def _split3(x):
    hi = x.astype(BF16)
    r = x - hi.astype(F32)
    mid = r.astype(BF16)
    lo = (r - mid.astype(F32)).astype(BF16)
    return hi, mid, lo


def _zero_after(v):
    bits = v[0:SUBLANES, 0:LANES].astype(jnp.int32)
    return lax.shift_right_logical(lax.shift_right_logical(bits, 16), 16).astype(F32)


def _gelu_tanh(x):
    c = math.sqrt(2.0 / math.pi)
    half = 0.5 * x
    return half + half * jnp.tanh(x * (c + (c * 0.044715) * (x * x)))


def _in_group(x):
    rows, C = x.shape
    x3 = x.reshape(rows // SUBLANES, SUBLANES, C)
    return x3, lax.broadcasted_iota(jnp.int32, x3.shape, 1)


def _head_norm_t(t, g_ref):
    tm = t.shape[1]
    outs = []
    for hh in range(ATT_HEADS):
        th = t[hh * HEAD_DIM:(hh + 1) * HEAD_DIM]
        outs.append(th * lax.rsqrt(jnp.mean(th * th, axis=0, keepdims=True) + NORM_EPS))
    return jnp.concatenate(outs, axis=0) * jnp.tile(g_ref[...], (1, tm // LANES))


def _proj_kernel(x_ref, g1_ref, wqkv_ref, wf_ref, wlru_ref, gq_ref, gk_ref, bf_ref,
                 selk_ref, selq_ref, kone_ref, cw_ref, cb_ref, wa_ref, ba_ref, wx_ref, bx_ref,
                 lam_ref,
                 qt_ref, qe_ref, kx_ref, vt_ref, a_ref, u_ref, gg_ref,
                 carry_ref, ext_ref, xc_ref, pa_ref, px_ref):
    tm = x_ref.shape[1]

    @pl.when(pl.program_id(1) == 0)
    def _():
        carry_ref[...] = jnp.zeros_like(carry_ref)
        ext_ref[...] = jnp.zeros_like(ext_ref)

    h = _rms(x_ref[0], g1_ref[...]).astype(BF16)
    C = a_ref.shape[-1]
    aw = ATT_WIDTH

    lx = _dot(h, wlru_ref[:, 0:C])
    f_logit = _dot(h, wf_ref[...])
    q_raw = _dot(h, wqkv_ref[:, 0:aw])

    x3, rin = _in_group(lx)
    halo = ext_ref[...]
    ext_ref[...] = lx[tm - SUBLANES:tm, :]
    xc3 = cb_ref[...] + x3 * cw_ref[CONV_WIDTH - 1:CONV_WIDTH, :]
    for k in range(1, CONV_WIDTH):
        rolled = pltpu.roll(x3, k, axis=1)
        before = jnp.concatenate([pltpu.roll(halo, k, axis=0)[None], rolled[:-1]], axis=0)
        tap = CONV_WIDTH - 1 - k
        xc3 = xc3 + jnp.where(rin >= k, rolled, before) * cw_ref[tap:tap + 1, :]
    xc = xc3.reshape(tm, C)
    xc_ref[...] = xc
    xcb = xc.astype(BF16)
    pa_ref[...] = _dot(xcb, wa_ref[...]) + ba_ref[...]
    px_ref[...] = _dot(xcb, wx_ref[...]) + bx_ref[...]

    nparts = 8
    decay_rate = -LRU_C * jax.nn.softplus(-lam_ref[...])

    def gates(part, after):
        rows = slice(part * (tm // nparts), (part + 1) * (tm // nparts))
        pre_a = pa_ref[rows, :]
        pre_a = pre_a + jnp.tile(_zero_after(after), (pre_a.shape[0] // SUBLANES, C // LANES))
        log_a = jax.nn.sigmoid(pre_a) * decay_rate
        gi = jax.nn.sigmoid(px_ref[rows, :])
        a = jnp.exp(log_a)
        a_ref[0, rows, :] = a
        y = -jnp.tanh(log_a) * (1.0 + a * a)
        root = jnp.where(y > 0.0, y * lax.rsqrt(y), 0.0)
        u_ref[0, rows, :] = root * (gi * xc_ref[rows, :])

    k_raw = _dot(h, wqkv_ref[:, aw:2 * aw])
    gates(0, k_raw)
    gates(1, k_raw)
    v_raw = _dot(h, wqkv_ref[:, 2 * aw:3 * aw])
    gates(2, v_raw)
    gates(3, v_raw)
    g_raw = _dot(h, wlru_ref[:, C:2 * C])
    for part in range(4, nparts):
        gates(part, g_raw)
    gg_ref[0] = _gelu_tanh(g_raw).astype(BF16)

    q = _head_norm_t(q_raw.T, gq_ref) * (LOG2E / math.sqrt(HEAD_DIM))
    qt_ref[0] = q.astype(BF16)
    k = _head_norm_t(k_raw.T, gk_ref).T.astype(BF16)
    vt_ref[0, 0] = v_raw.T.astype(BF16)

    logf = jax.nn.log_sigmoid(f_logit + bf_ref[...])
    c3, rin = _in_group(logf)
    for sh in (1, 2, 4):
        c3 = c3 + jnp.where(rin >= sh, pltpu.roll(c3, sh, axis=1), 0.0)
    prev = carry_ref[...]
    groups = []
    for gidx in range(tm // SUBLANES):
        cg = c3[gidx] + prev
        groups.append(cg)
        prev = jnp.broadcast_to(cg[SUBLANES - 1:SUBLANES, :], cg.shape)
    carry_ref[...] = prev
    cum = jnp.concatenate(groups, axis=0)

    hi, mid, lo = _split3(cum * LOG2E)
    lane = lax.broadcasted_iota(jnp.int32, hi.shape, 1)
    pieces = jnp.where(lane < ATT_HEADS, hi, jnp.where(lane < 2 * ATT_HEADS, mid, lo))
    ext_k = (_dot(pieces, selk_ref[...]) + kone_ref[...]).astype(BF16)
    for p in range(aw // HEAD_PAIR):
        pair = slice(p * HEAD_PAIR, (p + 1) * HEAD_PAIR)
        kx_ref[0, :, 2 * p * HEAD_PAIR:(2 * p + 1) * HEAD_PAIR] = k[:, pair]
        kx_ref[0, :, (2 * p + 1) * HEAD_PAIR:(2 * p + 2) * HEAD_PAIR] = ext_k[:, pair]
    qe_ref[0] = _dot_nt(selq_ref[...], pieces).astype(BF16)


def _proj_call(x, g1, wqkv, wf, wlru, gq, gk, bf, sel_k, sel_q, kone, cw, cb, wa, ba, wx, bx, lam,
               tm):
    B, S, D = x.shape
    aw = ATT_WIDTH
    npair = aw // HEAD_PAIR
    C = wlru.shape[1] // 2
    const = lambda shape: pl.BlockSpec(shape, lambda b, s: (0,) * len(shape))
    tok = lambda w: pl.BlockSpec((1, tm, w), lambda b, s: (b, s, 0))
    return pl.pallas_call(
        _proj_kernel,
        grid=(B, S // tm),
        in_specs=[tok(D), const((1, D)), const(wqkv.shape), const(wf.shape),
                  const(wlru.shape), const((aw, LANES)), const((aw, LANES)), const((1, LANES)),
                  const(sel_k.shape), const(sel_q.shape), const((1, aw)),
                  const(cw.shape), const((1, C)), const(wa.shape), const((1, C)),
                  const(wx.shape), const((1, C)), const((1, C))],
        out_specs=[pl.BlockSpec((1, aw, tm), lambda b, s: (b, 0, s)),
                   pl.BlockSpec((1, npair * BIAS_ROWS, tm), lambda b, s: (b, 0, s)),
                   tok(2 * aw),
                   pl.BlockSpec((1, 1, aw, tm), lambda b, s: (b, s, 0, 0)),
                   tok(C), tok(C), tok(C)],
        out_shape=[jax.ShapeDtypeStruct((B, aw, S), BF16),
                   jax.ShapeDtypeStruct((B, npair * BIAS_ROWS, S), BF16),
                   jax.ShapeDtypeStruct((B, S, 2 * aw), BF16),
                   jax.ShapeDtypeStruct((B, S // tm, aw, tm), BF16),
                   jax.ShapeDtypeStruct((B, S, C), F32),
                   jax.ShapeDtypeStruct((B, S, C), F32),
                   jax.ShapeDtypeStruct((B, S, C), BF16)],
        scratch_shapes=[pltpu.VMEM((SUBLANES, LANES), F32), pltpu.VMEM((SUBLANES, C), F32),
                        pltpu.VMEM((tm, C), F32), pltpu.VMEM((tm, C), F32),
                        pltpu.VMEM((tm, C), F32)],
        compiler_params=pltpu.CompilerParams(
            dimension_semantics=("arbitrary", "arbitrary"),
            vmem_limit_bytes=VMEM_LIMIT_BYTES),
        name="proj",
    )(x, g1, wqkv, wf, wlru, gq, gk, bf, sel_k, sel_q, kone, cw, cb, wa, ba, wx, bx, lam)


def _stacked_queries(qt, qe, direct):
    tq = qt.shape[1]
    zeros = jnp.zeros((HEAD_DIM, tq), BF16)
    pad = jnp.zeros((HEAD_PAIR - BIAS_ROWS, tq), BF16)
    erow = lax.broadcasted_iota(jnp.int32, (BIAS_ROWS, tq), 0)

    def bias_rows(h):
        rows = jnp.where((erow >= 3 * h) & (erow < 3 * h + 3), -1.0, 0.0).astype(BF16)
        if direct:
            rows = jnp.where((erow >= 6 + 3 * h) & (erow < 9 + 3 * h), qe, rows)
        return rows

    return (jnp.concatenate([qt[0:HEAD_DIM], zeros, bias_rows(0), pad], axis=0),
            jnp.concatenate([zeros, qt[HEAD_DIM:HEAD_PAIR], bias_rows(1), pad], axis=0))


def _softmax_step(state, s, vb, head, direct):
    m, l, acc = state
    rows = slice(head * HEAD_DIM, (head + 1) * HEAD_DIM)
    if direct:
        p = jnp.exp2(s)
        l = l + jnp.sum(p, axis=0, keepdims=True)
        return None, l, acc + _dot(vb, p.astype(BF16))[rows]
    m_new = jnp.maximum(m, jnp.max(s, axis=0, keepdims=True))
    alpha = jnp.exp2(m - m_new)
    p = jnp.exp2(s - m_new)
    l = alpha * l + jnp.sum(p, axis=0, keepdims=True)
    return m_new, l, alpha * acc + _dot(vb, p.astype(BF16))[rows]


def _attn_kernel(qt_ref, qe_ref, kx_ref, vt_ref, o_ref, *, tq, tk, td, direct):
    i = pl.program_id(2)
    qs = _stacked_queries(qt_ref[0], qe_ref[0], direct)
    qst = jnp.concatenate(qs, axis=1)
    nk = tq // tk

    def full_step(j, carry):
        kb = kx_ref[0, pl.ds(pl.multiple_of(j * tq, tq), tq), :]
        vb = jnp.concatenate([vt_ref[0, j * nk + n, 0] for n in range(nk)], axis=1)
        s2 = _dot(kb, qst)
        return tuple(_softmax_step(carry[h], s2[:, h * tq:(h + 1) * tq], vb, h, direct)
                     for h in range(2))

    init = tuple((None if direct else jnp.full((1, tq), -jnp.inf, F32), jnp.zeros((1, tq), F32),
                  jnp.zeros((HEAD_DIM, tq), F32)) for _ in range(2))
    carry = lax.fori_loop(0, i, full_step, init)

    krow = lax.broadcasted_iota(jnp.int32, (td, td), 0)
    qcol = lax.broadcasted_iota(jnp.int32, (td, td), 1)
    keep = krow <= qcol
    for c in range(tq // td):
        c0 = c * td
        w = tq - c0
        kb = kx_ref[0, pl.ds(pl.multiple_of(i * tq + c0, td), td), :]
        lo = c0 % tk
        vb = vt_ref[0, i * nk + c0 // tk, 0, :, lo:lo + td]
        s2 = _dot(kb, jnp.concatenate([qs[0][:, c0:], qs[1][:, c0:]], axis=1))
        new = []
        for h in range(2):
            s = s2[:, h * w:(h + 1) * w]
            sd = jnp.where(keep, s[:, 0:td], -jnp.inf)
            s = sd if w == td else jnp.concatenate([sd, s[:, td:]], axis=1)
            old = carry[h]
            part = _softmax_step(tuple(None if t is None else t[:, c0:] for t in old), s, vb, h,
                                 direct)
            if c0:
                part = tuple(None if t is None else jnp.concatenate([t[:, 0:c0], u], axis=1)
                             for t, u in zip(old, part))
            new.append(part)
        carry = tuple(new)

    (_, la, acca), (_, lb, accb) = carry
    ot = jnp.concatenate([acca / la, accb / lb], axis=0)
    o_ref[0] = ot.T.astype(o_ref.dtype)


def _attn_call(qt, qe, kx, vt, tq, tk, td, direct):
    B, aw, S = qt.shape
    npair = aw // HEAD_PAIR
    vt5 = vt.reshape(B, S // tk, npair, HEAD_PAIR, tk)
    return pl.pallas_call(
        functools.partial(_attn_kernel, tq=tq, tk=tk, td=td, direct=direct),
        grid=(B, npair, S // tq),
        in_specs=[pl.BlockSpec((1, HEAD_PAIR, tq), lambda b, p, i: (b, p, i)),
                  pl.BlockSpec((1, BIAS_ROWS, tq), lambda b, p, i: (b, p, i)),
                  pl.BlockSpec((1, S, 2 * HEAD_PAIR), lambda b, p, i: (b, 0, p)),
                  pl.BlockSpec((1, S // tk, 1, HEAD_PAIR, tk), lambda b, p, i: (b, 0, p, 0, 0))],
        out_specs=pl.BlockSpec((1, tq, HEAD_PAIR), lambda b, p, i: (b, i, p)),
        out_shape=jax.ShapeDtypeStruct((B, S, aw), BF16),
        compiler_params=pltpu.CompilerParams(
            dimension_semantics=("arbitrary", "arbitrary", "arbitrary"),
            vmem_limit_bytes=VMEM_LIMIT_BYTES),
        name="attn_direct" if direct else "attn_online",
    )(qt, qe, kx, vt5)


def _direct_softmax_is_safe(norm1_g, w_v, q_norm_g, k_norm_g):
    d_model = norm1_g.shape[0]
    qk_bits = LOG2E * math.sqrt(HEAD_DIM) * jnp.max(jnp.abs(q_norm_g)) * jnp.max(jnp.abs(k_norm_g))
    v_bound = (math.sqrt(d_model) * jnp.max(jnp.abs(norm1_g))
               * jnp.max(jnp.sqrt(jnp.sum(jnp.square(w_v.astype(F32)), axis=0))))
    return qk_bits + jnp.log2(jnp.maximum(v_bound, 1.0)) < DIRECT_SOFTMAX_MAX_BITS


def _lru_piece(rows, hprev, a_ref, u_ref, gg_ref, g_ref, rec_ref, after):
    a = a_ref[0, rows, :]
    u = u_ref[0, rows, :]
    pr, C = a.shape
    if after is not None:
        u = u + jnp.tile(_zero_after(after), (pr // SUBLANES, C // LANES))

    a, rin = _in_group(a)
    u, _ = _in_group(u)
    for sh in (1, 2, 4):
        ok = rin >= sh
        a_prev = jnp.where(ok, pltpu.roll(a, sh, axis=1), 1.0)
        u_prev = jnp.where(ok, pltpu.roll(u, sh, axis=1), 0.0)
        u = a * u_prev + u
        a = a * a_prev

    hs = []
    for gidx in range(pr // SUBLANES):
        hg = a[gidx] * hprev + u[gidx]
        hs.append(hg)
        hprev = jnp.broadcast_to(hg[SUBLANES - 1:SUBLANES, :], hg.shape)
    rec = jnp.concatenate(hs, axis=0) * gg_ref[0, rows, :].astype(F32)
    rec_ref[rows, :] = _rms(rec, g_ref[...]).astype(BF16)
    return hprev


def _mix_kernel(x_ref, att_ref, a_ref, u_ref, gg_ref, gl_ref, ga_ref, wo_ref, g2_ref, wg_ref,
                wu_ref, wd_ref, o_ref, act_ref, rec_ref, hc_ref, *, tf, pr, tiles_per_seq, n_tiles):
    g = pl.program_id(0)

    @pl.when(g == 0)
    def _():
        rec_ref[...] = jnp.zeros_like(rec_ref)
        hc_ref[...] = jnp.zeros_like(hc_ref)

    aw = att_ref.shape[-1]
    attn = _rms(att_ref[0].astype(F32), ga_ref[...]).astype(BF16)
    x1 = x_ref[0] + _dot(attn, wo_ref[0:aw, :]) + _dot(rec_ref[...], wo_ref[aw:, :])
    h2 = _rms(x1, g2_ref[...]).astype(BF16)

    first = (jnp.minimum(g, n_tiles - 1) % tiles_per_seq) == 0
    hprev = jnp.where(first, 0.0, hc_ref[...])

    tm = x_ref.shape[1]
    dff = wg_ref.shape[1]
    prev_act = None
    nchunk, npiece = dff // tf, tm // pr
    for c in range(nchunk):
        cols = slice(c * tf, (c + 1) * tf)
        act = jax.nn.silu(_dot(h2, wg_ref[:, cols])) * _dot(h2, wu_ref[:, cols])
        act_ref[:, cols] = act.astype(BF16)
        for p in range(npiece):
            if p * nchunk // npiece == c:
                hprev = _lru_piece(slice(p * pr, (p + 1) * pr), hprev, a_ref, u_ref, gg_ref,
                                   gl_ref, rec_ref, prev_act)
        prev_act = act
    hc_ref[...] = hprev
    o_ref[0] = x1 + _dot(act_ref[...], wd_ref[...])


def _mix_call(x, att, a, u, gg, gl, ga, wo, g2, wg, wu, wd, tm, tf, pr):
    B, S, D = x.shape
    aw = att.shape[-1]
    C = a.shape[-1]
    nS = S // tm
    N = B * nS
    const = lambda shape: pl.BlockSpec(shape, lambda g: (0,) * len(shape),
                                       pipeline_mode=pl.Buffered(1))

    def cur(w):
        return pl.BlockSpec((1, tm, w), lambda g: (jnp.minimum(g, N - 1) // nS,
                                                   jnp.minimum(g, N - 1) % nS, 0))

    def prev(w):
        return pl.BlockSpec((1, tm, w), lambda g: (jnp.maximum(g - 1, 0) // nS,
                                                   jnp.maximum(g - 1, 0) % nS, 0))

    return pl.pallas_call(
        functools.partial(_mix_kernel, tf=tf, pr=pr, tiles_per_seq=nS, n_tiles=N),
        grid=(N + 1,),
        in_specs=[prev(D), prev(aw), cur(C), cur(C), cur(C), const((1, C)), const((1, aw)),
                  const(wo.shape), const((1, D)), const(wg.shape), const(wu.shape),
                  const(wd.shape)],
        out_specs=prev(D),
        out_shape=jax.ShapeDtypeStruct((B, S, D), x.dtype),
        scratch_shapes=[pltpu.VMEM((tm, wg.shape[1]), BF16), pltpu.VMEM((tm, C), BF16),
                        pltpu.VMEM((SUBLANES, C), F32)],
        compiler_params=pltpu.CompilerParams(
            dimension_semantics=("arbitrary",),
            vmem_limit_bytes=VMEM_LIMIT_BYTES),
        name="mix",
    )(x, att, a, u, gg, gl, ga, wo, g2, wg, wu, wd)


def _block_diag(w):
    n, d, e = w.shape
    eye = jnp.eye(n, dtype=w.dtype)
    return jnp.einsum('nde,nm->ndme', w, eye).reshape(n * d, n * e)


def _bias_select():
    npair = ATT_WIDTH // HEAD_PAIR
    src = jnp.arange(LANES)
    piece, h = src // ATT_HEADS, src % ATT_HEADS
    slot = 3 * (h % 2) + piece
    valid = src < 3 * ATT_HEADS
    sel_k = ((jnp.arange(ATT_WIDTH)[None, :] == ((h // 2) * HEAD_PAIR + slot)[:, None])
             & valid[:, None])
    sel_q = ((jnp.arange(npair * BIAS_ROWS)[:, None] == ((h // 2) * BIAS_ROWS + 6 + slot)[None, :])
             & valid[None, :])
    in_pair = jnp.arange(ATT_WIDTH) % HEAD_PAIR
    kone = ((in_pair >= 6) & (in_pair < 12)).astype(F32).reshape(1, ATT_WIDTH)
    return sel_k.astype(BF16), sel_q.astype(BF16), kone


def _tiles(S):
    tm = min(512, S)
    tq = min(1024, S)
    td = min(256, S)
    tf = 256
    pr = min(16, S)
    return tm, tq, td, tf, pr


def _layer(x, norm1_g, w_in, q_norm_g, k_norm_g, b_f, conv_w, conv_b, w_a, b_a, w_x, b_x,
           lam, attn_out_g, lru_out_g, w_out, norm2_g, w_gate, w_up, w_down):
    B, S, D = x.shape
    aw = ATT_WIDTH
    tm, tq, td, tf, pr = _tiles(S)
    row = lambda a: a.reshape(1, -1).astype(F32)
    lane_pad = lambda a: jnp.pad(a, ((0, 0), (0, LANES - a.shape[1])))
    per_row = lambda g: jnp.broadcast_to(jnp.tile(g.astype(F32), ATT_HEADS)[:, None], (aw, LANES))

    wqkv = w_in[:, :3 * aw].astype(BF16)
    wf = lane_pad(jnp.tile(w_in[:, 3 * aw:3 * aw + ATT_HEADS], (1, 3))).astype(BF16)
    bf = lane_pad(jnp.tile(b_f.astype(F32).reshape(1, ATT_HEADS), (1, 3)))
    wlru = w_in[:, 3 * aw + ATT_HEADS:].astype(BF16)
    sel_k, sel_q, kone = _bias_select()

    qt, qe, kx, vt, a, u, gg = _proj_call(
        x, row(norm1_g), wqkv, wf, wlru, per_row(q_norm_g), per_row(k_norm_g), bf, sel_k, sel_q,
        kone, conv_w.astype(F32), row(conv_b), _block_diag(w_a).astype(BF16), row(b_a),
        _block_diag(w_x).astype(BF16), row(b_x), row(lam), tm)
    att = lax.cond(
        _direct_softmax_is_safe(norm1_g, w_in[:, 2 * aw:3 * aw], q_norm_g, k_norm_g),
        functools.partial(_attn_call, tq=tq, tk=tm, td=td, direct=True),
        functools.partial(_attn_call, tq=tq, tk=tm, td=td, direct=False),
        qt, qe, kx, vt)
    return _mix_call(x, att, a, u, gg, row(lru_out_g), row(attn_out_g), w_out.astype(BF16),
                     row(norm2_g), w_gate.astype(BF16), w_up.astype(BF16), w_down.astype(BF16),
                     tm, tf, pr)


def kernel(x, norm1_g, w_in, q_norm_g, k_norm_g, b_f, conv_w, conv_b, w_a, b_a, w_x, b_x, lam,
           attn_out_g, lru_out_g, w_out, norm2_g, w_gate, w_up, w_down):
    depth = norm1_g.shape[0]
    for l in range(depth):
        x = _layer(x, norm1_g[l], w_in[l], q_norm_g[l], k_norm_g[l], b_f[l], conv_w[l],
                   conv_b[l], w_a[l], b_a[l], w_x[l], b_x[l], lam[l], attn_out_g[l],
                   lru_out_g[l], w_out[l], norm2_g[l], w_gate[l], w_up[l], w_down[l])
    return x
```

```python
import functools
import math

import jax
import jax.numpy as jnp
from jax import lax
from jax.experimental import pallas as pl
from jax.experimental.pallas import tpu as pltpu

ATT_HEADS = 8
HEAD_DIM = 64
ATT_WIDTH = ATT_HEADS * HEAD_DIM
CONV_WIDTH = 4
LRU_C = 8.0
NORM_EPS = 1e-6
LOG2E = 1.4426950408889634

LANES = 128
SUBLANES = 8
HEAD_PAIR = 2 * HEAD_DIM
BIAS_ROWS = 16
DIRECT_SOFTMAX_MAX_BITS = 100.0
VMEM_LIMIT_BYTES = 56 * 1024 * 1024

F32 = jnp.float32
BF16 = jnp.bfloat16


def _dot(a, b):
    return jnp.dot(a, b, preferred_element_type=F32)


def _dot_nt(a, b):
    return lax.dot_general(a, b, (((1,), (1,)), ((), ())), preferred_element_type=F32)


def _rms(xf, g):
    return xf * lax.rsqrt(jnp.mean(xf * xf, axis=-1, keepdims=True) + NORM_EPS) * g


def _split3(x):
    hi = x.astype(BF16)
    r = x - hi.astype(F32)
    mid = r.astype(BF16)
    lo = (r - mid.astype(F32)).astype(BF16)
    return hi, mid, lo


def _zero_after(v):
    bits = v[0:SUBLANES, 0:LANES].astype(jnp.int32)
    return lax.shift_right_logical(lax.shift_right_logical(bits, 16), 16).astype(F32)


def _gelu_tanh(x):
    c = math.sqrt(2.0 / math.pi)
    half = 0.5 * x
    return half + half * jnp.tanh(x * (c + (c * 0.044715) * (x * x)))


def _in_group(x):
    rows, C = x.shape
    x3 = x.reshape(rows // SUBLANES, SUBLANES, C)
    return x3, lax.broadcasted_iota(jnp.int32, x3.shape, 1)


def _head_norm_t(t, g_ref):
    tm = t.shape[1]
    outs = []
    for hh in range(ATT_HEADS):
        th = t[hh * HEAD_DIM:(hh + 1) * HEAD_DIM]
        outs.append(th * lax.rsqrt(jnp.mean(th * th, axis=0, keepdims=True) + NORM_EPS))
    return jnp.concatenate(outs, axis=0) * jnp.tile(g_ref[...], (1, tm // LANES))


def _proj_kernel(x_ref, g1_ref, wqkv_ref, wf_ref, wlru_ref, gq_ref, gk_ref, bf_ref,
                 selk_ref, selq_ref, kone_ref, cw_ref, cb_ref, wa_ref, ba_ref, wx_ref, bx_ref,
                 lam_ref,
                 qt_ref, qe_ref, kx_ref, vt_ref, a_ref, u_ref, gg_ref,
                 carry_ref, halo_ref, xc_ref, pa_ref, px_ref, gr_ref,
                 *, tiles_per_seq, n_tiles):
    g = pl.program_id(0)
    tm = x_ref.shape[1]
    C = a_ref.shape[-1]
    aw = ATT_WIDTH
    new, old = g % 2, 1 - g % 2

    @pl.when(g == 0)
    def _():
        for ref in (carry_ref, halo_ref, xc_ref, pa_ref, px_ref, gr_ref):
            ref[...] = jnp.zeros_like(ref)

    repeat = g == n_tiles
    first = (jnp.minimum(g, n_tiles - 1) % tiles_per_seq) == 0
    state = lambda ref: jnp.where(first, 0.0, jnp.where(repeat, ref[1], ref[0]))
    carry_in, halo = state(carry_ref), state(halo_ref)
    carry_ref[1] = carry_in
    halo_ref[1] = halo

    h = _rms(x_ref[0], g1_ref[...]).astype(BF16)

    lx = _dot(h, wlru_ref[:, 0:C])
    f_logit = _dot(h, wf_ref[...])
    q_raw = _dot(h, wqkv_ref[:, 0:aw])
    k_raw = _dot(h, wqkv_ref[:, aw:2 * aw])

    logf = jax.nn.log_sigmoid(f_logit + bf_ref[...])
    c3, rin = _in_group(logf)
    for sh in (1, 2, 4):
        c3 = c3 + jnp.where(rin >= sh, pltpu.roll(c3, sh, axis=1), 0.0)
    prev = carry_in
    groups = []
    for gidx in range(tm // SUBLANES):
        cg = c3[gidx] + prev
        groups.append(cg)
        prev = jnp.broadcast_to(cg[SUBLANES - 1:SUBLANES, :], cg.shape)
    carry_ref[0] = prev
    cum = jnp.concatenate(groups, axis=0)
    hi, mid, lo = _split3(cum * LOG2E)
    lane = lax.broadcasted_iota(jnp.int32, hi.shape, 1)
    pieces = jnp.where(lane < ATT_HEADS, hi, jnp.where(lane < 2 * ATT_HEADS, mid, lo))

    x3, rin = _in_group(lx)
    halo_ref[0] = lx[tm - SUBLANES:tm, :]
    xc3 = cb_ref[...] + x3 * cw_ref[CONV_WIDTH - 1:CONV_WIDTH, :]
    for k in range(1, CONV_WIDTH):
        rolled = pltpu.roll(x3, k, axis=1)
        before = jnp.concatenate([pltpu.roll(halo, k, axis=0)[None], rolled[:-1]], axis=0)
        tap = CONV_WIDTH - 1 - k
        xc3 = xc3 + jnp.where(rin >= k, rolled, before) * cw_ref[tap:tap + 1, :]
    xc = xc3.reshape(tm, C)
    xc_ref[new] = xc
    xcb = xc.astype(BF16)

    nparts = 8
    decay_rate = -LRU_C * jax.nn.softplus(-lam_ref[...])
    for part in range(nparts):
        rows = slice(part * (tm // nparts), (part + 1) * (tm // nparts))
        log_a = jax.nn.sigmoid(pa_ref[old, rows, :]) * decay_rate
        gi = jax.nn.sigmoid(px_ref[old, rows, :])
        a = jnp.exp(log_a)
        a_ref[0, rows, :] = a
        y = -jnp.tanh(log_a) * (1.0 + a * a)
        root = jnp.where(y > 0.0, y * lax.rsqrt(y), 0.0)
        u_ref[0, rows, :] = root * (gi * xc_ref[old, rows, :])
        gg_ref[0, rows, :] = _gelu_tanh(gr_ref[old, rows, :]).astype(BF16)

    v_raw = _dot(h, wqkv_ref[:, 2 * aw:3 * aw])
    ext_k = (_dot(pieces, selk_ref[...]) + kone_ref[...]).astype(BF16)
    qe_ref[0] = _dot_nt(selq_ref[...], pieces).astype(BF16)
    gr_ref[new] = _dot(h, wlru_ref[:, C:2 * C])
    pa_ref[new] = _dot(xcb, wa_ref[...]) + ba_ref[...]
    px_ref[new] = _dot(xcb, wx_ref[...]) + bx_ref[...]

    q = _head_norm_t(q_raw.T, gq_ref) * (LOG2E / math.sqrt(HEAD_DIM))
    qt_ref[0] = q.astype(BF16)
    k = _head_norm_t(k_raw.T, gk_ref).T.astype(BF16)
    vt_ref[0, 0] = v_raw.T.astype(BF16)
    for p in range(aw // HEAD_PAIR):
        pair = slice(p * HEAD_PAIR, (p + 1) * HEAD_PAIR)
        kx_ref[0, :, 2 * p * HEAD_PAIR:(2 * p + 1) * HEAD_PAIR] = k[:, pair]
        kx_ref[0, :, (2 * p + 1) * HEAD_PAIR:(2 * p + 2) * HEAD_PAIR] = ext_k[:, pair]


def _proj_call(x, g1, wqkv, wf, wlru, gq, gk, bf, sel_k, sel_q, kone, cw, cb, wa, ba, wx, bx, lam,
               tm):
    B, S, D = x.shape
    aw = ATT_WIDTH
    npair = aw // HEAD_PAIR
    C = wlru.shape[1] // 2
    nS = S // tm
    N = B * nS
    const = lambda shape: pl.BlockSpec(shape, lambda g: (0,) * len(shape))
    cur = lambda g: (jnp.minimum(g, N - 1) // nS, jnp.minimum(g, N - 1) % nS)
    prev = lambda g: (jnp.maximum(g - 1, 0) // nS, jnp.maximum(g - 1, 0) % nS)
    tok = lambda w, tile: pl.BlockSpec((1, tm, w), lambda g: (*tile(g), 0))
    return pl.pallas_call(
        functools.partial(_proj_kernel, tiles_per_seq=nS, n_tiles=N),
        grid=(N + 1,),
        in_specs=[tok(D, cur), const((1, D)), const(wqkv.shape), const(wf.shape),
                  const(wlru.shape), const((aw, LANES)), const((aw, LANES)), const((1, LANES)),
                  const(sel_k.shape), const(sel_q.shape), const((1, aw)),
                  const(cw.shape), const((1, C)), const(wa.shape), const((1, C)),
                  const(wx.shape), const((1, C)), const((1, C))],
        out_specs=[pl.BlockSpec((1, aw, tm), lambda g: (cur(g)[0], 0, cur(g)[1])),
                   pl.BlockSpec((1, npair * BIAS_ROWS, tm), lambda g: (cur(g)[0], 0, cur(g)[1])),
                   tok(2 * aw, cur),
                   pl.BlockSpec((1, 1, aw, tm), lambda g: (*cur(g), 0, 0)),
                   tok(C, prev), tok(C, prev), tok(C, prev)],
        out_shape=[jax.ShapeDtypeStruct((B, aw, S), BF16),
                   jax.ShapeDtypeStruct((B, npair * BIAS_ROWS, S), BF16),
                   jax.ShapeDtypeStruct((B, S, 2 * aw), BF16),
                   jax.ShapeDtypeStruct((B, S // tm, aw, tm), BF16),
                   jax.ShapeDtypeStruct((B, S, C), F32),
                   jax.ShapeDtypeStruct((B, S, C), F32),
                   jax.ShapeDtypeStruct((B, S, C), BF16)],
        scratch_shapes=[pltpu.VMEM((2, SUBLANES, LANES), F32), pltpu.VMEM((2, SUBLANES, C), F32)]
        + [pltpu.VMEM((2, tm, C), F32)] * 4,
        compiler_params=pltpu.CompilerParams(
            dimension_semantics=("arbitrary",),
            vmem_limit_bytes=VMEM_LIMIT_BYTES),
        name="proj",
    )(x, g1, wqkv, wf, wlru, gq, gk, bf, sel_k, sel_q, kone, cw, cb, wa, ba, wx, bx, lam)


def _stacked_queries(qt, qe, direct):
    tq = qt.shape[1]
    zeros = jnp.zeros((HEAD_DIM, tq), BF16)
    pad = jnp.zeros((HEAD_PAIR - BIAS_ROWS, tq), BF16)
    erow = lax.broadcasted_iota(jnp.int32, (BIAS_ROWS, tq), 0)

    def bias_rows(h):
        rows = jnp.where((erow >= 3 * h) & (erow < 3 * h + 3), -1.0, 0.0).astype(BF16)
        if direct:
            rows = jnp.where((erow >= 6 + 3 * h) & (erow < 9 + 3 * h), qe, rows)
        return rows

    return (jnp.concatenate([qt[0:HEAD_DIM], zeros, bias_rows(0), pad], axis=0),
            jnp.concatenate([zeros, qt[HEAD_DIM:HEAD_PAIR], bias_rows(1), pad], axis=0))


def _softmax_step(state, s, vb, head, direct):
    m, l, acc = state
    rows = slice(head * HEAD_DIM, (head + 1) * HEAD_DIM)
    if direct:
        p = jnp.exp2(s)
        l = l + jnp.sum(p, axis=0, keepdims=True)
        return None, l, acc + _dot(vb, p.astype(BF16))[rows]
    m_new = jnp.maximum(m, jnp.max(s, axis=0, keepdims=True))
    alpha = jnp.exp2(m - m_new)
    p = jnp.exp2(s - m_new)
    l = alpha * l + jnp.sum(p, axis=0, keepdims=True)
    return m_new, l, alpha * acc + _dot(vb, p.astype(BF16))[rows]


def _attn_kernel(qt_ref, qe_ref, kx_ref, vt_ref, o_ref, *, tq, tk, td, direct):
    i = pl.program_id(2)
    qs = _stacked_queries(qt_ref[0], qe_ref[0], direct)
    qst = jnp.concatenate(qs, axis=1)
    nk = tq // tk

    def full_step(j, carry):
        kb = kx_ref[0, pl.ds(pl.multiple_of(j * tq, tq), tq), :]
        vb = jnp.concatenate([vt_ref[0, j * nk + n, 0] for n in range(nk)], axis=1)
        s2 = _dot(kb, qst)
        return tuple(_softmax_step(carry[h], s2[:, h * tq:(h + 1) * tq], vb, h, direct)
                     for h in range(2))

    init = tuple((None if direct else jnp.full((1, tq), -jnp.inf, F32), jnp.zeros((1, tq), F32),
                  jnp.zeros((HEAD_DIM, tq), F32)) for _ in range(2))
    carry = lax.fori_loop(0, i, full_step, init)

    krow = lax.broadcasted_iota(jnp.int32, (td, td), 0)
    qcol = lax.broadcasted_iota(jnp.int32, (td, td), 1)
    keep = krow <= qcol
    scores = []
    for c in range(tq // td):
        c0 = c * td
        kb = kx_ref[0, pl.ds(pl.multiple_of(i * tq + c0, td), td), :]
        scores.append(_dot(kb, jnp.concatenate([qs[0][:, c0:], qs[1][:, c0:]], axis=1)))
    for c in range(tq // td):
        c0 = c * td
        w = tq - c0
        lo = c0 % tk
        vb = vt_ref[0, i * nk + c0 // tk, 0, :, lo:lo + td]
        s2 = scores[c]
        new = []
        for h in range(2):
            s = s2[:, h * w:(h + 1) * w]
            sd = jnp.where(keep, s[:, 0:td], -jnp.inf)
            s = sd if w == td else jnp.concatenate([sd, s[:, td:]], axis=1)
            old = carry[h]
            part = _softmax_step(tuple(None if t is None else t[:, c0:] for t in old), s, vb, h,
                                 direct)
            if c0:
                part = tuple(None if t is None else jnp.concatenate([t[:, 0:c0], u], axis=1)
                             for t, u in zip(old, part))
            new.append(part)
        carry = tuple(new)

    (_, la, acca), (_, lb, accb) = carry
    ot = jnp.concatenate([acca / la, accb / lb], axis=0)
    o_ref[0] = ot.T.astype(o_ref.dtype)


def _attn_call(qt, qe, kx, vt, tq, tk, td, direct):
    B, aw, S = qt.shape
    npair = aw // HEAD_PAIR
    vt5 = vt.reshape(B, S // tk, npair, HEAD_PAIR, tk)
    return pl.pallas_call(
        functools.partial(_attn_kernel, tq=tq, tk=tk, td=td, direct=direct),
        grid=(B, npair, S // tq),
        in_specs=[pl.BlockSpec((1, HEAD_PAIR, tq), lambda b, p, i: (b, p, i)),
                  pl.BlockSpec((1, BIAS_ROWS, tq), lambda b, p, i: (b, p, i)),
                  pl.BlockSpec((1, S, 2 * HEAD_PAIR), lambda b, p, i: (b, 0, p)),
                  pl.BlockSpec((1, S // tk, 1, HEAD_PAIR, tk), lambda b, p, i: (b, 0, p, 0, 0))],
        out_specs=pl.BlockSpec((1, tq, HEAD_PAIR), lambda b, p, i: (b, i, p)),
        out_shape=jax.ShapeDtypeStruct((B, S, aw), BF16),
        compiler_params=pltpu.CompilerParams(
            dimension_semantics=("arbitrary", "arbitrary", "arbitrary"),
            vmem_limit_bytes=VMEM_LIMIT_BYTES),
        name="attn_direct" if direct else "attn_online",
    )(qt, qe, kx, vt5)


def _direct_softmax_is_safe(norm1_g, w_v, q_norm_g, k_norm_g):
    d_model = norm1_g.shape[0]
    qk_bits = LOG2E * math.sqrt(HEAD_DIM) * jnp.max(jnp.abs(q_norm_g)) * jnp.max(jnp.abs(k_norm_g))
    v_bound = (math.sqrt(d_model) * jnp.max(jnp.abs(norm1_g))
               * jnp.max(jnp.sqrt(jnp.sum(jnp.square(w_v.astype(F32)), axis=0))))
    return qk_bits + jnp.log2(jnp.maximum(v_bound, 1.0)) < DIRECT_SOFTMAX_MAX_BITS


def _lru_piece(rows, hprev, a_ref, u_ref, gg_ref, g_ref, rec_ref, after):
    a = a_ref[0, rows, :]
    u = u_ref[0, rows, :]
    pr, C = a.shape
    if after is not None:
        u = u + jnp.tile(_zero_after(after), (pr // SUBLANES, C // LANES))

    a, rin = _in_group(a)
    u, _ = _in_group(u)
    for sh in (1, 2, 4):
        ok = rin >= sh
        a_prev = jnp.where(ok, pltpu.roll(a, sh, axis=1), 1.0)
        u_prev = jnp.where(ok, pltpu.roll(u, sh, axis=1), 0.0)
        u = a * u_prev + u
        a = a * a_prev

    hs = []
    for gidx in range(pr // SUBLANES):
        hg = a[gidx] * hprev + u[gidx]
        hs.append(hg)
        hprev = jnp.broadcast_to(hg[SUBLANES - 1:SUBLANES, :], hg.shape)
    rec = jnp.concatenate(hs, axis=0) * gg_ref[0, rows, :].astype(F32)
    rec_ref[rows, :] = _rms(rec, g_ref[...]).astype(BF16)
    return hprev


def _mix_kernel(x_ref, att_ref, a_ref, u_ref, gg_ref, gl_ref, ga_ref, wo_ref, g2_ref, wg_ref,
                wu_ref, wd_ref, o_ref, act_ref, rec_ref, hc_ref, *, tf, pr, tiles_per_seq, n_tiles):
    g = pl.program_id(0)

    @pl.when(g == 0)
    def _():
        rec_ref[...] = jnp.zeros_like(rec_ref)
        hc_ref[...] = jnp.zeros_like(hc_ref)

    aw = att_ref.shape[-1]
    attn = _rms(att_ref[0].astype(F32), ga_ref[...]).astype(BF16)
    x1 = x_ref[0] + _dot(attn, wo_ref[0:aw, :]) + _dot(rec_ref[...], wo_ref[aw:, :])
    h2 = _rms(x1, g2_ref[...]).astype(BF16)

    first = (jnp.minimum(g, n_tiles - 1) % tiles_per_seq) == 0
    hprev = jnp.where(first, 0.0, hc_ref[...])

    tm = x_ref.shape[1]
    dff = wg_ref.shape[1]
    prev_act = None
    nchunk, npiece = dff // tf, tm // pr
    for c in range(nchunk):
        cols = slice(c * tf, (c + 1) * tf)
        act = jax.nn.silu(_dot(h2, wg_ref[:, cols])) * _dot(h2, wu_ref[:, cols])
        act_ref[:, cols] = act.astype(BF16)
        for p in range(npiece):
            if p * nchunk // npiece == c:
                hprev = _lru_piece(slice(p * pr, (p + 1) * pr), hprev, a_ref, u_ref, gg_ref,
                                   gl_ref, rec_ref, prev_act)
        prev_act = act
    hc_ref[...] = hprev
    o_ref[0] = x1 + _dot(act_ref[...], wd_ref[...])


def _mix_call(x, att, a, u, gg, gl, ga, wo, g2, wg, wu, wd, tm, tf, pr):
    B, S, D = x.shape
    aw = att.shape[-1]
    C = a.shape[-1]
    nS = S // tm
    N = B * nS
    const = lambda shape: pl.BlockSpec(shape, lambda g: (0,) * len(shape),
                                       pipeline_mode=pl.Buffered(1))

    def cur(w):
        return pl.BlockSpec((1, tm, w), lambda g: (jnp.minimum(g, N - 1) // nS,
                                                   jnp.minimum(g, N - 1) % nS, 0))

    def prev(w):
        return pl.BlockSpec((1, tm, w), lambda g: (jnp.maximum(g - 1, 0) // nS,
                                                   jnp.maximum(g - 1, 0) % nS, 0))

    return pl.pallas_call(
        functools.partial(_mix_kernel, tf=tf, pr=pr, tiles_per_seq=nS, n_tiles=N),
        grid=(N + 1,),
        in_specs=[prev(D), prev(aw), cur(C), cur(C), cur(C), const((1, C)), const((1, aw)),
                  const(wo.shape), const((1, D)), const(wg.shape), const(wu.shape),
                  const(wd.shape)],
        out_specs=prev(D),
        out_shape=jax.ShapeDtypeStruct((B, S, D), x.dtype),
        scratch_shapes=[pltpu.VMEM((tm, wg.shape[1]), BF16), pltpu.VMEM((tm, C), BF16),
                        pltpu.VMEM((SUBLANES, C), F32)],
        compiler_params=pltpu.CompilerParams(
            dimension_semantics=("arbitrary",),
            vmem_limit_bytes=VMEM_LIMIT_BYTES),
        name="mix",
    )(x, att, a, u, gg, gl, ga, wo, g2, wg, wu, wd)


def _block_diag(w):
    n, d, e = w.shape
    eye = jnp.eye(n, dtype=w.dtype)
    return jnp.einsum('nde,nm->ndme', w, eye).reshape(n * d, n * e)


def _bias_select():
    npair = ATT_WIDTH // HEAD_PAIR
    src = jnp.arange(LANES)
    piece, h = src // ATT_HEADS, src % ATT_HEADS
    slot = 3 * (h % 2) + piece
    valid = src < 3 * ATT_HEADS
    sel_k = ((jnp.arange(ATT_WIDTH)[None, :] == ((h // 2) * HEAD_PAIR + slot)[:, None])
             & valid[:, None])
    sel_q = ((jnp.arange(npair * BIAS_ROWS)[:, None] == ((h // 2) * BIAS_ROWS + 6 + slot)[None, :])
             & valid[None, :])
    in_pair = jnp.arange(ATT_WIDTH) % HEAD_PAIR
    kone = ((in_pair >= 6) & (in_pair < 12)).astype(F32).reshape(1, ATT_WIDTH)
    return sel_k.astype(BF16), sel_q.astype(BF16), kone


def _tiles(S):
    tm = min(512, S)
    tq = min(1024, S)
    td = min(256, S)
    tf = 256
    pr = min(16, S)
    return tm, tq, td, tf, pr


def _layer(x, norm1_g, w_in, q_norm_g, k_norm_g, b_f, conv_w, conv_b, w_a, b_a, w_x, b_x,
           lam, attn_out_g, lru_out_g, w_out, norm2_g, w_gate, w_up, w_down):
    B, S, D = x.shape
    aw = ATT_WIDTH
    tm, tq, td, tf, pr = _tiles(S)
    row = lambda a: a.reshape(1, -1).astype(F32)
    lane_pad = lambda a: jnp.pad(a, ((0, 0), (0, LANES - a.shape[1])))
    per_row = lambda g: jnp.broadcast_to(jnp.tile(g.astype(F32), ATT_HEADS)[:, None], (aw, LANES))

    wqkv = w_in[:, :3 * aw].astype(BF16)
    wf = lane_pad(jnp.tile(w_in[:, 3 * aw:3 * aw + ATT_HEADS], (1, 3))).astype(BF16)
    bf = lane_pad(jnp.tile(b_f.astype(F32).reshape(1, ATT_HEADS), (1, 3)))
    wlru = w_in[:, 3 * aw + ATT_HEADS:].astype(BF16)
    sel_k, sel_q, kone = _bias_select()

    qt, qe, kx, vt, a, u, gg = _proj_call(
        x, row(norm1_g), wqkv, wf, wlru, per_row(q_norm_g), per_row(k_norm_g), bf, sel_k, sel_q,
        kone, conv_w.astype(F32), row(conv_b), _block_diag(w_a).astype(BF16), row(b_a),
        _block_diag(w_x).astype(BF16), row(b_x), row(lam), tm)
    att = lax.cond(
        _direct_softmax_is_safe(norm1_g, w_in[:, 2 * aw:3 * aw], q_norm_g, k_norm_g),
        functools.partial(_attn_call, tq=tq, tk=tm, td=td, direct=True),
        functools.partial(_attn_call, tq=tq, tk=tm, td=td, direct=False),
        qt, qe, kx, vt)
    return _mix_call(x, att, a, u, gg, row(lru_out_g), row(attn_out_g), w_out.astype(BF16),
                     row(norm2_g), w_gate.astype(BF16), w_up.astype(BF16), w_down.astype(BF16),
                     tm, tf, pr)


def kernel(x, norm1_g, w_in, q_norm_g, k_norm_g, b_f, conv_w, conv_b, w_a, b_a, w_x, b_x, lam,
           attn_out_g, lru_out_g, w_out, norm2_g, w_gate, w_up, w_down):
    depth = norm1_g.shape[0]
    for l in range(depth):
        x = _layer(x, norm1_g[l], w_in[l], q_norm_g[l], k_norm_g[l], b_f[l], conv_w[l],
                   conv_b[l], w_a[l], b_a[l], w_x[l], b_x[l], lam[l], attn_out_g[l],
                   lru_out_g[l], w_out[l], norm2_g[l], w_gate[l], w_up[l], w_down[l])
    return x
```

```python
import functools
import math

import jax
import jax.numpy as jnp
from jax import lax
from jax.experimental import pallas as pl
from jax.experimental.pallas import tpu as pltpu

ATT_HEADS = 8
HEAD_DIM = 64
ATT_WIDTH = ATT_HEADS * HEAD_DIM
CONV_WIDTH = 4
LRU_C = 8.0
NORM_EPS = 1e-6
LOG2E = 1.4426950408889634

LANES = 128
SUBLANES = 8
HEAD_PAIR = 2 * HEAD_DIM
BIAS_ROWS = 16
DIRECT_SOFTMAX_MAX_BITS = 100.0
VMEM_LIMIT_BYTES = 56 * 1024 * 1024

F32 = jnp.float32
BF16 = jnp.bfloat16


def _dot(a, b):
    return jnp.dot(a, b, preferred_element_type=F32)


def _dot_nt(a, b):
    return lax.dot_general(a, b, (((1,), (1,)), ((), ())), preferred_element_type=F32)


def _rms(xf, g):
    return xf * lax.rsqrt(jnp.mean(xf * xf, axis=-1, keepdims=True) + NORM_EPS) * g


def _split3(x):
    hi = x.astype(BF16)
    r = x - hi.astype(F32)
    mid = r.astype(BF16)
    lo = (r - mid.astype(F32)).astype(BF16)
    return hi, mid, lo


def _zero_after(v):
    bits = v[0:SUBLANES, 0:LANES].astype(jnp.int32)
    return lax.shift_right_logical(lax.shift_right_logical(bits, 16), 16).astype(F32)


def _gelu_tanh(x):
    c = math.sqrt(2.0 / math.pi)
    half = 0.5 * x
    return half + half * jnp.tanh(x * (c + (c * 0.044715) * (x * x)))


def _in_group(x):
    rows, C = x.shape
    x3 = x.reshape(rows // SUBLANES, SUBLANES, C)
    return x3, lax.broadcasted_iota(jnp.int32, x3.shape, 1)


def _head_norm_t(t, g_ref):
    tm = t.shape[1]
    outs = []
    for hh in range(ATT_HEADS):
        th = t[hh * HEAD_DIM:(hh + 1) * HEAD_DIM]
        outs.append(th * lax.rsqrt(jnp.mean(th * th, axis=0, keepdims=True) + NORM_EPS))
    return jnp.concatenate(outs, axis=0) * jnp.tile(g_ref[...], (1, tm // LANES))


def _proj_kernel(x_ref, g1_ref, wqkv_ref, wf_ref, wlru_ref, gq_ref, gk_ref, bf_ref,
                 selk_ref, selq_ref, kone_ref, cw_ref, cb_ref, wa_ref, ba_ref, wx_ref, bx_ref,
                 lam_ref,
                 qt_ref, qe_ref, kx_ref, vt_ref, a_ref, u_ref, gg_ref,
                 carry_ref, halo_ref, xc_ref, pa_ref, px_ref, gr_ref,
                 *, tiles_per_seq, n_tiles):
    g = pl.program_id(0)
    tm = x_ref.shape[1]
    C = a_ref.shape[-1]
    aw = ATT_WIDTH
    new, old = g % 2, 1 - g % 2

    @pl.when(g == 0)
    def _():
        for ref in (carry_ref, halo_ref, xc_ref, pa_ref, px_ref, gr_ref):
            ref[...] = jnp.zeros_like(ref)

    repeat = g == n_tiles
    first = (jnp.minimum(g, n_tiles - 1) % tiles_per_seq) == 0
    state = lambda ref: jnp.where(first, 0.0, jnp.where(repeat, ref[1], ref[0]))
    carry_in, halo = state(carry_ref), state(halo_ref)
    carry_ref[1] = carry_in
    halo_ref[1] = halo

    h = _rms(x_ref[0], g1_ref[...]).astype(BF16)

    lx = _dot(h, wlru_ref[:, 0:C])
    f_logit = _dot(h, wf_ref[...])
    q_raw = _dot(h, wqkv_ref[:, 0:aw])
    k_raw = _dot(h, wqkv_ref[:, aw:2 * aw])

    logf = jax.nn.log_sigmoid(f_logit + bf_ref[...])
    c3, rin = _in_group(logf)
    for sh in (1, 2, 4):
        c3 = c3 + jnp.where(rin >= sh, pltpu.roll(c3, sh, axis=1), 0.0)
    prev = carry_in
    groups = []
    for gidx in range(tm // SUBLANES):
        cg = c3[gidx] + prev
        groups.append(cg)
        prev = jnp.broadcast_to(cg[SUBLANES - 1:SUBLANES, :], cg.shape)
    carry_ref[0] = prev
    cum = jnp.concatenate(groups, axis=0)
    hi, mid, lo = _split3(cum * LOG2E)
    lane = lax.broadcasted_iota(jnp.int32, hi.shape, 1)
    pieces = jnp.where(lane < ATT_HEADS, hi, jnp.where(lane < 2 * ATT_HEADS, mid, lo))

    x3, rin = _in_group(lx)
    halo_ref[0] = lx[tm - SUBLANES:tm, :]
    xc3 = cb_ref[...] + x3 * cw_ref[CONV_WIDTH - 1:CONV_WIDTH, :]
    for k in range(1, CONV_WIDTH):
        rolled = pltpu.roll(x3, k, axis=1)
        before = jnp.concatenate([pltpu.roll(halo, k, axis=0)[None], rolled[:-1]], axis=0)
        tap = CONV_WIDTH - 1 - k
        xc3 = xc3 + jnp.where(rin >= k, rolled, before) * cw_ref[tap:tap + 1, :]
    xc = xc3.reshape(tm, C)
    xc_ref[new] = xc
    xcb = xc.astype(BF16)

    nparts = 8
    decay_rate = -LRU_C * jax.nn.softplus(-lam_ref[...])
    for part in range(nparts):
        rows = slice(part * (tm // nparts), (part + 1) * (tm // nparts))
        log_a = jax.nn.sigmoid(pa_ref[old, rows, :]) * decay_rate
        gi = jax.nn.sigmoid(px_ref[old, rows, :])
        a = jnp.exp(log_a)
        a_ref[0, rows, :] = a
        y = -jnp.tanh(log_a) * (1.0 + a * a)
        root = jnp.where(y > 0.0, y * lax.rsqrt(y), 0.0)
        u_ref[0, rows, :] = root * (gi * xc_ref[old, rows, :])
        gg_ref[0, rows, :] = _gelu_tanh(gr_ref[old, rows, :]).astype(BF16)

    v_raw = _dot(h, wqkv_ref[:, 2 * aw:3 * aw])
    ext_k = (_dot(pieces, selk_ref[...]) + kone_ref[...]).astype(BF16)
    qe_ref[0] = _dot_nt(selq_ref[...], pieces).astype(BF16)
    gr_ref[new] = _dot(h, wlru_ref[:, C:2 * C])
    pa_ref[new] = _dot(xcb, wa_ref[...]) + ba_ref[...]
    px_ref[new] = _dot(xcb, wx_ref[...]) + bx_ref[...]

    q = _head_norm_t(q_raw.T, gq_ref) * (LOG2E / math.sqrt(HEAD_DIM))
    qt_ref[0] = q.astype(BF16)
    k = _head_norm_t(k_raw.T, gk_ref).T.astype(BF16)
    vt_ref[0, 0] = v_raw.T.astype(BF16)
    for p in range(aw // HEAD_PAIR):
        pair = slice(p * HEAD_PAIR, (p + 1) * HEAD_PAIR)
        kx_ref[0, :, 2 * p * HEAD_PAIR:(2 * p + 1) * HEAD_PAIR] = k[:, pair]
        kx_ref[0, :, (2 * p + 1) * HEAD_PAIR:(2 * p + 2) * HEAD_PAIR] = ext_k[:, pair]


def _proj_call(x, g1, wqkv, wf, wlru, gq, gk, bf, sel_k, sel_q, kone, cw, cb, wa, ba, wx, bx, lam,
               tm):
    B, S, D = x.shape
    aw = ATT_WIDTH
    npair = aw // HEAD_PAIR
    C = wlru.shape[1] // 2
    nS = S // tm
    N = B * nS
    const = lambda shape: pl.BlockSpec(shape, lambda g: (0,) * len(shape))
    cur = lambda g: (jnp.minimum(g, N - 1) // nS, jnp.minimum(g, N - 1) % nS)
    prev = lambda g: (jnp.maximum(g - 1, 0) // nS, jnp.maximum(g - 1, 0) % nS)
    tok = lambda w, tile: pl.BlockSpec((1, tm, w), lambda g: (*tile(g), 0))
    return pl.pallas_call(
        functools.partial(_proj_kernel, tiles_per_seq=nS, n_tiles=N),
        grid=(N + 1,),
        in_specs=[tok(D, cur), const((1, D)), const(wqkv.shape), const(wf.shape),
                  const(wlru.shape), const((aw, LANES)), const((aw, LANES)), const((1, LANES)),
                  const(sel_k.shape), const(sel_q.shape), const((1, aw)),
                  const(cw.shape), const((1, C)), const(wa.shape), const((1, C)),
                  const(wx.shape), const((1, C)), const((1, C))],
        out_specs=[pl.BlockSpec((1, aw, tm), lambda g: (cur(g)[0], 0, cur(g)[1])),
                   pl.BlockSpec((1, npair * BIAS_ROWS, tm), lambda g: (cur(g)[0], 0, cur(g)[1])),
                   tok(2 * aw, cur),
                   pl.BlockSpec((1, 1, aw, tm), lambda g: (*cur(g), 0, 0)),
                   tok(C, prev), tok(C, prev), tok(C, prev)],
        out_shape=[jax.ShapeDtypeStruct((B, aw, S), BF16),
                   jax.ShapeDtypeStruct((B, npair * BIAS_ROWS, S), BF16),
                   jax.ShapeDtypeStruct((B, S, 2 * aw), BF16),
                   jax.ShapeDtypeStruct((B, S // tm, aw, tm), BF16),
                   jax.ShapeDtypeStruct((B, S, C), F32),
                   jax.ShapeDtypeStruct((B, S, C), F32),
                   jax.ShapeDtypeStruct((B, S, C), BF16)],
        scratch_shapes=[pltpu.VMEM((2, SUBLANES, LANES), F32), pltpu.VMEM((2, SUBLANES, C), F32)]
        + [pltpu.VMEM((2, tm, C), F32)] * 4,
        compiler_params=pltpu.CompilerParams(
            dimension_semantics=("arbitrary",),
            vmem_limit_bytes=VMEM_LIMIT_BYTES),
        name="proj",
    )(x, g1, wqkv, wf, wlru, gq, gk, bf, sel_k, sel_q, kone, cw, cb, wa, ba, wx, bx, lam)


def _stacked_queries(qt, qe, direct):
    tq = qt.shape[1]
    zeros = jnp.zeros((HEAD_DIM, tq), BF16)
    pad = jnp.zeros((HEAD_PAIR - BIAS_ROWS, tq), BF16)
    erow = lax.broadcasted_iota(jnp.int32, (BIAS_ROWS, tq), 0)

    def bias_rows(h):
        rows = jnp.where((erow >= 3 * h) & (erow < 3 * h + 3), -1.0, 0.0).astype(BF16)
        if direct:
            rows = jnp.where((erow >= 6 + 3 * h) & (erow < 9 + 3 * h), qe, rows)
        return rows

    return (jnp.concatenate([qt[0:HEAD_DIM], zeros, bias_rows(0), pad], axis=0),
            jnp.concatenate([zeros, qt[HEAD_DIM:HEAD_PAIR], bias_rows(1), pad], axis=0))


def _softmax_step(state, s, vb, head, direct):
    m, l, acc = state
    rows = slice(head * HEAD_DIM, (head + 1) * HEAD_DIM)
    if direct:
        p = jnp.exp2(s)
        l = l + jnp.sum(p, axis=0, keepdims=True)
        return None, l, acc + _dot(vb, p.astype(BF16))[rows]
    m_new = jnp.maximum(m, jnp.max(s, axis=0, keepdims=True))
    alpha = jnp.exp2(m - m_new)
    p = jnp.exp2(s - m_new)
    l = alpha * l + jnp.sum(p, axis=0, keepdims=True)
    return m_new, l, alpha * acc + _dot(vb, p.astype(BF16))[rows]


def _attn_tile(i, qt_ref, qe_ref, kx_ref, vt_ref, o_ref, *, tq, tk, td, direct):
    qs = _stacked_queries(qt_ref[0], qe_ref[0], direct)
    qst = jnp.concatenate(qs, axis=1)
    nk = tq // tk

    def full_scores(j):
        return _dot(kx_ref[0, j * tq:(j + 1) * tq, :], qst)

    def diag_scores():
        return [_dot(kx_ref[0, i * tq + c * td:i * tq + (c + 1) * td, :],
                     jnp.concatenate([qs[0][:, c * td:], qs[1][:, c * td:]], axis=1))
                for c in range(tq // td)]

    carry = tuple((None if direct else jnp.full((1, tq), -jnp.inf, F32), jnp.zeros((1, tq), F32),
                   jnp.zeros((HEAD_DIM, tq), F32)) for _ in range(2))
    ahead = full_scores(0) if i else diag_scores()
    for j in range(i):
        s2 = ahead
        ahead = full_scores(j + 1) if j + 1 < i else diag_scores()
        vb = jnp.concatenate([vt_ref[0, j * nk + n, 0] for n in range(nk)], axis=1)
        carry = tuple(_softmax_step(carry[h], s2[:, h * tq:(h + 1) * tq], vb, h, direct)
                      for h in range(2))

    krow = lax.broadcasted_iota(jnp.int32, (td, td), 0)
    qcol = lax.broadcasted_iota(jnp.int32, (td, td), 1)
    keep = krow <= qcol
    for c, s2 in enumerate(ahead):
        c0 = c * td
        w = tq - c0
        lo = c0 % tk
        vb = vt_ref[0, i * nk + c0 // tk, 0, :, lo:lo + td]
        new = []
        for h in range(2):
            s = s2[:, h * w:(h + 1) * w]
            sd = jnp.where(keep, s[:, 0:td], -jnp.inf)
            s = sd if w == td else jnp.concatenate([sd, s[:, td:]], axis=1)
            old = carry[h]
            part = _softmax_step(tuple(None if t is None else t[:, c0:] for t in old), s, vb, h,
                                 direct)
            if c0:
                part = tuple(None if t is None else jnp.concatenate([t[:, 0:c0], u], axis=1)
                             for t, u in zip(old, part))
            new.append(part)
        carry = tuple(new)

    (_, la, acca), (_, lb, accb) = carry
    ot = jnp.concatenate([acca / la, accb / lb], axis=0)
    o_ref[0] = ot.T.astype(o_ref.dtype)


def _attn_kernel(qt_ref, qe_ref, kx_ref, vt_ref, o_ref, *, n_tiles, **static):
    for i in range(n_tiles):
        pl.when(pl.program_id(2) == i)(
            functools.partial(_attn_tile, i, qt_ref, qe_ref, kx_ref, vt_ref, o_ref, **static))


def _attn_call(qt, qe, kx, vt, tq, tk, td, direct):
    B, aw, S = qt.shape
    npair = aw // HEAD_PAIR
    vt5 = vt.reshape(B, S // tk, npair, HEAD_PAIR, tk)
    return pl.pallas_call(
        functools.partial(_attn_kernel, n_tiles=S // tq, tq=tq, tk=tk, td=td, direct=direct),
        grid=(B, npair, S // tq),
        in_specs=[pl.BlockSpec((1, HEAD_PAIR, tq), lambda b, p, i: (b, p, i)),
                  pl.BlockSpec((1, BIAS_ROWS, tq), lambda b, p, i: (b, p, i)),
                  pl.BlockSpec((1, S, 2 * HEAD_PAIR), lambda b, p, i: (b, 0, p)),
                  pl.BlockSpec((1, S // tk, 1, HEAD_PAIR, tk), lambda b, p, i: (b, 0, p, 0, 0))],
        out_specs=pl.BlockSpec((1, tq, HEAD_PAIR), lambda b, p, i: (b, i, p)),
        out_shape=jax.ShapeDtypeStruct((B, S, aw), BF16),
        compiler_params=pltpu.CompilerParams(
            dimension_semantics=("arbitrary", "arbitrary", "arbitrary"),
            vmem_limit_bytes=VMEM_LIMIT_BYTES),
        name="attn_direct" if direct else "attn_online",
    )(qt, qe, kx, vt5)


def _direct_softmax_is_safe(norm1_g, w_v, q_norm_g, k_norm_g):
    d_model = norm1_g.shape[0]
    qk_bits = LOG2E * math.sqrt(HEAD_DIM) * jnp.max(jnp.abs(q_norm_g)) * jnp.max(jnp.abs(k_norm_g))
    v_bound = (math.sqrt(d_model) * jnp.max(jnp.abs(norm1_g))
               * jnp.max(jnp.sqrt(jnp.sum(jnp.square(w_v.astype(F32)), axis=0))))
    return qk_bits + jnp.log2(jnp.maximum(v_bound, 1.0)) < DIRECT_SOFTMAX_MAX_BITS


def _lru_piece(rows, hprev, a_ref, u_ref, gg_ref, g_ref, rec_ref, after):
    a = a_ref[0, rows, :]
    u = u_ref[0, rows, :]
    pr, C = a.shape
    if after is not None:
        u = u + jnp.tile(_zero_after(after), (pr // SUBLANES, C // LANES))

    a, rin = _in_group(a)
    u, _ = _in_group(u)
    for sh in (1, 2, 4):
        ok = rin >= sh
        a_prev = jnp.where(ok, pltpu.roll(a, sh, axis=1), 1.0)
        u_prev = jnp.where(ok, pltpu.roll(u, sh, axis=1), 0.0)
        u = a * u_prev + u
        a = a * a_prev

    hs = []
    for gidx in range(pr // SUBLANES):
        hg = a[gidx] * hprev + u[gidx]
        hs.append(hg)
        hprev = jnp.broadcast_to(hg[SUBLANES - 1:SUBLANES, :], hg.shape)
    rec = jnp.concatenate(hs, axis=0) * gg_ref[0, rows, :].astype(F32)
    rec_ref[rows, :] = _rms(rec, g_ref[...]).astype(BF16)
    return hprev


def _mix_kernel(x_ref, att_ref, a_ref, u_ref, gg_ref, gl_ref, ga_ref, wo_ref, g2_ref, wg_ref,
                wu_ref, wd_ref, o_ref, act_ref, rec_ref, hc_ref, *, tf, pr, tiles_per_seq, n_tiles):
    g = pl.program_id(0)

    @pl.when(g == 0)
    def _():
        rec_ref[...] = jnp.zeros_like(rec_ref)
        hc_ref[...] = jnp.zeros_like(hc_ref)

    aw = att_ref.shape[-1]
    attn = _rms(att_ref[0].astype(F32), ga_ref[...]).astype(BF16)
    x1 = x_ref[0] + _dot(attn, wo_ref[0:aw, :]) + _dot(rec_ref[...], wo_ref[aw:, :])
    h2 = _rms(x1, g2_ref[...]).astype(BF16)

    first = (jnp.minimum(g, n_tiles - 1) % tiles_per_seq) == 0
    hprev = jnp.where(first, 0.0, hc_ref[...])

    tm = x_ref.shape[1]
    dff = wg_ref.shape[1]
    prev_act = None
    nchunk, npiece = dff // tf, tm // pr
    for c in range(nchunk):
        cols = slice(c * tf, (c + 1) * tf)
        act = jax.nn.silu(_dot(h2, wg_ref[:, cols])) * _dot(h2, wu_ref[:, cols])
        act_ref[:, cols] = act.astype(BF16)
        for p in range(npiece):
            if p * nchunk // npiece == c:
                hprev = _lru_piece(slice(p * pr, (p + 1) * pr), hprev, a_ref, u_ref, gg_ref,
                                   gl_ref, rec_ref, prev_act)
        prev_act = act
    hc_ref[...] = hprev
    o_ref[0] = x1 + _dot(act_ref[...], wd_ref[...])


def _mix_call(x, att, a, u, gg, gl, ga, wo, g2, wg, wu, wd, tm, tf, pr):
    B, S, D = x.shape
    aw = att.shape[-1]
    C = a.shape[-1]
    nS = S // tm
    N = B * nS
    const = lambda shape: pl.BlockSpec(shape, lambda g: (0,) * len(shape),
                                       pipeline_mode=pl.Buffered(1))

    def cur(w):
        return pl.BlockSpec((1, tm, w), lambda g: (jnp.minimum(g, N - 1) // nS,
                                                   jnp.minimum(g, N - 1) % nS, 0))

    def prev(w):
        return pl.BlockSpec((1, tm, w), lambda g: (jnp.maximum(g - 1, 0) // nS,
                                                   jnp.maximum(g - 1, 0) % nS, 0))

    return pl.pallas_call(
        functools.partial(_mix_kernel, tf=tf, pr=pr, tiles_per_seq=nS, n_tiles=N),
        grid=(N + 1,),
        in_specs=[prev(D), prev(aw), cur(C), cur(C), cur(C), const((1, C)), const((1, aw)),
                  const(wo.shape), const((1, D)), const(wg.shape), const(wu.shape),
                  const(wd.shape)],
        out_specs=prev(D),
        out_shape=jax.ShapeDtypeStruct((B, S, D), x.dtype),
        scratch_shapes=[pltpu.VMEM((tm, wg.shape[1]), BF16), pltpu.VMEM((tm, C), BF16),
                        pltpu.VMEM((SUBLANES, C), F32)],
        compiler_params=pltpu.CompilerParams(
            dimension_semantics=("arbitrary",),
            vmem_limit_bytes=VMEM_LIMIT_BYTES),
        name="mix",
    )(x, att, a, u, gg, gl, ga, wo, g2, wg, wu, wd)


def _block_diag(w):
    n, d, e = w.shape
    eye = jnp.eye(n, dtype=w.dtype)
    return jnp.einsum('nde,nm->ndme', w, eye).reshape(n * d, n * e)


def _bias_select():
    npair = ATT_WIDTH // HEAD_PAIR
    src = jnp.arange(LANES)
    piece, h = src // ATT_HEADS, src % ATT_HEADS
    slot = 3 * (h % 2) + piece
    valid = src < 3 * ATT_HEADS
    sel_k = ((jnp.arange(ATT_WIDTH)[None, :] == ((h // 2) * HEAD_PAIR + slot)[:, None])
             & valid[:, None])
    sel_q = ((jnp.arange(npair * BIAS_ROWS)[:, None] == ((h // 2) * BIAS_ROWS + 6 + slot)[None, :])
             & valid[None, :])
    in_pair = jnp.arange(ATT_WIDTH) % HEAD_PAIR
    kone = ((in_pair >= 6) & (in_pair < 12)).astype(F32).reshape(1, ATT_WIDTH)
    return sel_k.astype(BF16), sel_q.astype(BF16), kone


def _tiles(S):
    tp = min(1024, S)
    tm = min(512, S)
    tq = min(1024, S)
    td = min(256, S)
    tf = 256
    pr = min(16, S)
    return tp, tm, tq, td, tf, pr


def _layer(x, norm1_g, w_in, q_norm_g, k_norm_g, b_f, conv_w, conv_b, w_a, b_a, w_x, b_x,
           lam, attn_out_g, lru_out_g, w_out, norm2_g, w_gate, w_up, w_down):
    B, S, D = x.shape
    aw = ATT_WIDTH
    tp, tm, tq, td, tf, pr = _tiles(S)
    row = lambda a: a.reshape(1, -1).astype(F32)
    lane_pad = lambda a: jnp.pad(a, ((0, 0), (0, LANES - a.shape[1])))
    per_row = lambda g: jnp.broadcast_to(jnp.tile(g.astype(F32), ATT_HEADS)[:, None], (aw, LANES))

    wqkv = w_in[:, :3 * aw].astype(BF16)
    wf = lane_pad(jnp.tile(w_in[:, 3 * aw:3 * aw + ATT_HEADS], (1, 3))).astype(BF16)
    bf = lane_pad(jnp.tile(b_f.astype(F32).reshape(1, ATT_HEADS), (1, 3)))
    wlru = w_in[:, 3 * aw + ATT_HEADS:].astype(BF16)
    sel_k, sel_q, kone = _bias_select()

    qt, qe, kx, vt, a, u, gg = _proj_call(
        x, row(norm1_g), wqkv, wf, wlru, per_row(q_norm_g), per_row(k_norm_g), bf, sel_k, sel_q,
        kone, conv_w.astype(F32), row(conv_b), _block_diag(w_a).astype(BF16), row(b_a),
        _block_diag(w_x).astype(BF16), row(b_x), row(lam), tp)
    att = lax.cond(
        _direct_softmax_is_safe(norm1_g, w_in[:, 2 * aw:3 * aw], q_norm_g, k_norm_g),
        functools.partial(_attn_call, tq=tq, tk=tp, td=td, direct=True),
        functools.partial(_attn_call, tq=tq, tk=tp, td=td, direct=False),
        qt, qe, kx, vt)
    return _mix_call(x, att, a, u, gg, row(lru_out_g), row(attn_out_g), w_out.astype(BF16),
                     row(norm2_g), w_gate.astype(BF16), w_up.astype(BF16), w_down.astype(BF16),
                     tm, tf, pr)


def kernel(x, norm1_g, w_in, q_norm_g, k_norm_g, b_f, conv_w, conv_b, w_a, b_a, w_x, b_x, lam,
           attn_out_g, lru_out_g, w_out, norm2_g, w_gate, w_up, w_down):
    depth = norm1_g.shape[0]
    for l in range(depth):
        x = _layer(x, norm1_g[l], w_in[l], q_norm_g[l], k_norm_g[l], b_f[l], conv_w[l],
                   conv_b[l], w_a[l], b_a[l], w_x[l], b_x[l], lam[l], attn_out_g[l],
                   lru_out_g[l], w_out[l], norm2_g[l], w_gate[l], w_up[l], w_down[l])
    return x
```

```python
import functools
import math

import jax
import jax.numpy as jnp
from jax import lax
from jax.experimental import pallas as pl
from jax.experimental.pallas import tpu as pltpu

ATT_HEADS = 8
HEAD_DIM = 64
ATT_WIDTH = ATT_HEADS * HEAD_DIM
CONV_WIDTH = 4
LRU_C = 8.0
NORM_EPS = 1e-6
LOG2E = 1.4426950408889634

LANES = 128
SUBLANES = 8
HEAD_PAIR = 2 * HEAD_DIM
BIAS_ROWS = 16
DIRECT_SOFTMAX_MAX_BITS = 100.0
VMEM_LIMIT_BYTES = 56 * 1024 * 1024

F32 = jnp.float32
BF16 = jnp.bfloat16


def _dot(a, b):
    return jnp.dot(a, b, preferred_element_type=F32)


def _dot_nt(a, b):
    return lax.dot_general(a, b, (((1,), (1,)), ((), ())), preferred_element_type=F32)


def _rms(xf, g):
    return xf * lax.rsqrt(jnp.mean(xf * xf, axis=-1, keepdims=True) + NORM_EPS) * g


def _split3(x):
    hi = x.astype(BF16)
    r = x - hi.astype(F32)
    mid = r.astype(BF16)
    lo = (r - mid.astype(F32)).astype(BF16)
    return hi, mid, lo


def _zero_after(v):
    bits = v[0:SUBLANES, 0:LANES].astype(jnp.int32)
    return lax.shift_right_logical(lax.shift_right_logical(bits, 16), 16).astype(F32)


def _gelu_tanh(x):
    c = math.sqrt(2.0 / math.pi)
    half = 0.5 * x
    return half + half * jnp.tanh(x * (c + (c * 0.044715) * (x * x)))


def _in_group(x):
    rows, C = x.shape
    x3 = x.reshape(rows // SUBLANES, SUBLANES, C)
    return x3, lax.broadcasted_iota(jnp.int32, x3.shape, 1)


def _head_norm_t(t, g_ref):
    tm = t.shape[1]
    outs = []
    for hh in range(ATT_HEADS):
        th = t[hh * HEAD_DIM:(hh + 1) * HEAD_DIM]
        outs.append(th * lax.rsqrt(jnp.mean(th * th, axis=0, keepdims=True) + NORM_EPS))
    return jnp.concatenate(outs, axis=0) * jnp.tile(g_ref[...], (1, tm // LANES))


def _proj_kernel(x_ref, g1_ref, wqkv_ref, wf_ref, wlru_ref, gq_ref, gk_ref, bf_ref,
                 selk_ref, selq_ref, kone_ref, cw_ref, cb_ref, wa_ref, ba_ref, wx_ref, bx_ref,
                 lam_ref,
                 qt_ref, qe_ref, kx_ref, vt_ref, a_ref, u_ref, gg_ref,
                 carry_ref, halo_ref, xc_ref, pa_ref, px_ref, gr_ref,
                 *, tiles_per_seq, n_tiles):
    g = pl.program_id(0)
    tm = x_ref.shape[1]
    C = a_ref.shape[-1]
    aw = ATT_WIDTH
    new, old = g % 2, 1 - g % 2

    @pl.when(g == 0)
    def _():
        for ref in (carry_ref, halo_ref, xc_ref, pa_ref, px_ref, gr_ref):
            ref[...] = jnp.zeros_like(ref)

    repeat = g == n_tiles
    first = (jnp.minimum(g, n_tiles - 1) % tiles_per_seq) == 0
    state = lambda ref: jnp.where(first, 0.0, jnp.where(repeat, ref[1], ref[0]))
    carry_in, halo = state(carry_ref), state(halo_ref)
    carry_ref[1] = carry_in
    halo_ref[1] = halo

    h = _rms(x_ref[0], g1_ref[...]).astype(BF16)

    lx = _dot(h, wlru_ref[:, 0:C])
    f_logit = _dot(h, wf_ref[...])
    q_raw = _dot(h, wqkv_ref[:, 0:aw])
    k_raw = _dot(h, wqkv_ref[:, aw:2 * aw])

    logf = jax.nn.log_sigmoid(f_logit + bf_ref[...])
    c3, rin = _in_group(logf)
    for sh in (1, 2, 4):
        c3 = c3 + jnp.where(rin >= sh, pltpu.roll(c3, sh, axis=1), 0.0)
    prev = carry_in
    groups = []
    for gidx in range(tm // SUBLANES):
        cg = c3[gidx] + prev
        groups.append(cg)
        prev = jnp.broadcast_to(cg[SUBLANES - 1:SUBLANES, :], cg.shape)
    carry_ref[0] = prev
    cum = jnp.concatenate(groups, axis=0)
    hi, mid, lo = _split3(cum * LOG2E)
    lane = lax.broadcasted_iota(jnp.int32, hi.shape, 1)
    pieces = jnp.where(lane < ATT_HEADS, hi, jnp.where(lane < 2 * ATT_HEADS, mid, lo))

    x3, rin = _in_group(lx)
    halo_ref[0] = lx[tm - SUBLANES:tm, :]
    xc3 = cb_ref[...] + x3 * cw_ref[CONV_WIDTH - 1:CONV_WIDTH, :]
    for k in range(1, CONV_WIDTH):
        rolled = pltpu.roll(x3, k, axis=1)
        before = jnp.concatenate([pltpu.roll(halo, k, axis=0)[None], rolled[:-1]], axis=0)
        tap = CONV_WIDTH - 1 - k
        xc3 = xc3 + jnp.where(rin >= k, rolled, before) * cw_ref[tap:tap + 1, :]
    xc = xc3.reshape(tm, C)
    xc_ref[new] = xc
    xcb = xc.astype(BF16)

    nparts = 8
    decay_rate = -LRU_C * jax.nn.softplus(-lam_ref[...])
    for part in range(nparts):
        rows = slice(part * (tm // nparts), (part + 1) * (tm // nparts))
        log_a = jax.nn.sigmoid(pa_ref[old, rows, :]) * decay_rate
        gi = jax.nn.sigmoid(px_ref[old, rows, :])
        a = jnp.exp(log_a)
        a_ref[0, rows, :] = a
        y = -jnp.tanh(log_a) * (1.0 + a * a)
        root = jnp.where(y > 0.0, y * lax.rsqrt(y), 0.0)
        u_ref[0, rows, :] = root * (gi * xc_ref[old, rows, :])
        gg_ref[0, rows, :] = _gelu_tanh(gr_ref[old, rows, :]).astype(BF16)

    v_raw = _dot(h, wqkv_ref[:, 2 * aw:3 * aw])
    ext_k = (_dot(pieces, selk_ref[...]) + kone_ref[...]).astype(BF16)
    qe_ref[0] = _dot_nt(selq_ref[...], pieces).astype(BF16)
    gr_ref[new] = _dot(h, wlru_ref[:, C:2 * C])
    pa_ref[new] = _dot(xcb, wa_ref[...]) + ba_ref[...]
    px_ref[new] = _dot(xcb, wx_ref[...]) + bx_ref[...]

    q = _head_norm_t(q_raw.T, gq_ref) * (LOG2E / math.sqrt(HEAD_DIM))
    qt_ref[0] = q.astype(BF16)
    k = _head_norm_t(k_raw.T, gk_ref).T.astype(BF16)
    vt_ref[0, 0] = v_raw.T.astype(BF16).reshape(vt_ref.shape[2:])
    for p in range(aw // HEAD_PAIR):
        pair = slice(p * HEAD_PAIR, (p + 1) * HEAD_PAIR)
        kx_ref[0, :, 2 * p * HEAD_PAIR:(2 * p + 1) * HEAD_PAIR] = k[:, pair]
        kx_ref[0, :, (2 * p + 1) * HEAD_PAIR:(2 * p + 2) * HEAD_PAIR] = ext_k[:, pair]


def _proj_call(x, g1, wqkv, wf, wlru, gq, gk, bf, sel_k, sel_q, kone, cw, cb, wa, ba, wx, bx, lam,
               tm):
    B, S, D = x.shape
    aw = ATT_WIDTH
    npair = aw // HEAD_PAIR
    C = wlru.shape[1] // 2
    nS = S // tm
    N = B * nS
    const = lambda shape: pl.BlockSpec(shape, lambda g: (0,) * len(shape))
    cur = lambda g: (jnp.minimum(g, N - 1) // nS, jnp.minimum(g, N - 1) % nS)
    prev = lambda g: (jnp.maximum(g - 1, 0) // nS, jnp.maximum(g - 1, 0) % nS)
    tok = lambda w, tile: pl.BlockSpec((1, tm, w), lambda g: (*tile(g), 0))
    return pl.pallas_call(
        functools.partial(_proj_kernel, tiles_per_seq=nS, n_tiles=N),
        grid=(N + 1,),
        in_specs=[tok(D, cur), const((1, D)), const(wqkv.shape), const(wf.shape),
                  const(wlru.shape), const((aw, LANES)), const((aw, LANES)), const((1, LANES)),
                  const(sel_k.shape), const(sel_q.shape), const((1, aw)),
                  const(cw.shape), const((1, C)), const(wa.shape), const((1, C)),
                  const(wx.shape), const((1, C)), const((1, C))],
        out_specs=[pl.BlockSpec((1, aw, tm), lambda g: (cur(g)[0], 0, cur(g)[1])),
                   pl.BlockSpec((1, npair * BIAS_ROWS, tm), lambda g: (cur(g)[0], 0, cur(g)[1])),
                   tok(2 * aw, cur),
                   pl.BlockSpec((1, 1, npair, HEAD_PAIR, tm), lambda g: (*cur(g), 0, 0, 0)),
                   tok(C, prev), tok(C, prev), tok(C, prev)],
        out_shape=[jax.ShapeDtypeStruct((B, aw, S), BF16),
                   jax.ShapeDtypeStruct((B, npair * BIAS_ROWS, S), BF16),
                   jax.ShapeDtypeStruct((B, S, 2 * aw), BF16),
                   jax.ShapeDtypeStruct((B, S // tm, npair, HEAD_PAIR, tm), BF16),
                   jax.ShapeDtypeStruct((B, S, C), F32),
                   jax.ShapeDtypeStruct((B, S, C), F32),
                   jax.ShapeDtypeStruct((B, S, C), BF16)],
        scratch_shapes=[pltpu.VMEM((2, SUBLANES, LANES), F32), pltpu.VMEM((2, SUBLANES, C), F32)]
        + [pltpu.VMEM((2, tm, C), F32)] * 4,
        compiler_params=pltpu.CompilerParams(
            dimension_semantics=("arbitrary",),
            vmem_limit_bytes=VMEM_LIMIT_BYTES),
        name="proj",
    )(x, g1, wqkv, wf, wlru, gq, gk, bf, sel_k, sel_q, kone, cw, cb, wa, ba, wx, bx, lam)


def _stacked_queries(qt, qe, direct):
    tq = qt.shape[1]
    zeros = jnp.zeros((HEAD_DIM, tq), BF16)
    pad = jnp.zeros((HEAD_PAIR - BIAS_ROWS, tq), BF16)
    erow = lax.broadcasted_iota(jnp.int32, (BIAS_ROWS, tq), 0)

    def bias_rows(h):
        rows = jnp.where((erow >= 3 * h) & (erow < 3 * h + 3), -1.0, 0.0).astype(BF16)
        if direct:
            rows = jnp.where((erow >= 6 + 3 * h) & (erow < 9 + 3 * h), qe, rows)
        return rows

    return (jnp.concatenate([qt[0:HEAD_DIM], zeros, bias_rows(0), pad], axis=0),
            jnp.concatenate([zeros, qt[HEAD_DIM:HEAD_PAIR], bias_rows(1), pad], axis=0))


def _softmax_step(state, s, vb, head, direct):
    m, l, acc = state
    rows = slice(head * HEAD_DIM, (head + 1) * HEAD_DIM)
    if direct:
        p = jnp.exp2(s)
        l = l + jnp.sum(p, axis=0, keepdims=True)
        return None, l, acc + _dot(vb, p.astype(BF16))[rows]
    m_new = jnp.maximum(m, jnp.max(s, axis=0, keepdims=True))
    alpha = jnp.exp2(m - m_new)
    p = jnp.exp2(s - m_new)
    l = alpha * l + jnp.sum(p, axis=0, keepdims=True)
    return m_new, l, alpha * acc + _dot(vb, p.astype(BF16))[rows]


def _attn_tile(i, qt_ref, qe_ref, kx_ref, vt_ref, o_ref, *, tq, tk, td, direct):
    qs = _stacked_queries(qt_ref[0], qe_ref[0], direct)
    qst = jnp.concatenate(qs, axis=1)
    nk = tq // tk

    def full_scores(j):
        return _dot(kx_ref[0, j * tq:(j + 1) * tq, :], qst)

    def diag_scores():
        return [_dot(kx_ref[0, i * tq + c * td:i * tq + (c + 1) * td, :],
                     jnp.concatenate([qs[0][:, c * td:], qs[1][:, c * td:]], axis=1))
                for c in range(tq // td)]

    carry = tuple((None if direct else jnp.full((1, tq), -jnp.inf, F32), jnp.zeros((1, tq), F32),
                   jnp.zeros((HEAD_DIM, tq), F32)) for _ in range(2))
    ahead = full_scores(0) if i else diag_scores()
    for j in range(i):
        s2 = ahead
        ahead = full_scores(j + 1) if j + 1 < i else diag_scores()
        vb = jnp.concatenate([vt_ref[0, j * nk + n, 0] for n in range(nk)], axis=1)
        carry = tuple(_softmax_step(carry[h], s2[:, h * tq:(h + 1) * tq], vb, h, direct)
                      for h in range(2))

    krow = lax.broadcasted_iota(jnp.int32, (td, td), 0)
    qcol = lax.broadcasted_iota(jnp.int32, (td, td), 1)
    keep = krow <= qcol
    for c, s2 in enumerate(ahead):
        c0 = c * td
        w = tq - c0
        lo = c0 % tk
        vb = vt_ref[0, i * nk + c0 // tk, 0, :, lo:lo + td]
        new = []
        for h in range(2):
            s = s2[:, h * w:(h + 1) * w]
            sd = jnp.where(keep, s[:, 0:td], -jnp.inf)
            s = sd if w == td else jnp.concatenate([sd, s[:, td:]], axis=1)
            old = carry[h]
            part = _softmax_step(tuple(None if t is None else t[:, c0:] for t in old), s, vb, h,
                                 direct)
            if c0:
                part = tuple(None if t is None else jnp.concatenate([t[:, 0:c0], u], axis=1)
                             for t, u in zip(old, part))
            new.append(part)
        carry = tuple(new)

    (_, la, acca), (_, lb, accb) = carry
    ot = jnp.concatenate([acca / la, accb / lb], axis=0)
    o_ref[0] = ot.T.astype(o_ref.dtype)


def _attn_kernel(qt_ref, qe_ref, kx_ref, vt_ref, o_ref, *, n_tiles, **static):
    for i in range(n_tiles):
        pl.when(pl.program_id(2) == i)(
            functools.partial(_attn_tile, i, qt_ref, qe_ref, kx_ref, vt_ref, o_ref, **static))


def _attn_call(qt, qe, kx, vt, tq, tk, td, direct):
    B, aw, S = qt.shape
    npair = aw // HEAD_PAIR
    assert vt.shape == (B, S // tk, npair, HEAD_PAIR, tk)
    return pl.pallas_call(
        functools.partial(_attn_kernel, n_tiles=S // tq, tq=tq, tk=tk, td=td, direct=direct),
        grid=(B, npair, S // tq),
        in_specs=[pl.BlockSpec((1, HEAD_PAIR, tq), lambda b, p, i: (b, p, i)),
                  pl.BlockSpec((1, BIAS_ROWS, tq), lambda b, p, i: (b, p, i)),
                  pl.BlockSpec((1, S, 2 * HEAD_PAIR), lambda b, p, i: (b, 0, p)),
                  pl.BlockSpec((1, S // tk, 1, HEAD_PAIR, tk), lambda b, p, i: (b, 0, p, 0, 0))],
        out_specs=pl.BlockSpec((1, tq, HEAD_PAIR), lambda b, p, i: (b, i, p)),
        out_shape=jax.ShapeDtypeStruct((B, S, aw), BF16),
        compiler_params=pltpu.CompilerParams(
            dimension_semantics=("arbitrary", "arbitrary", "arbitrary"),
            vmem_limit_bytes=VMEM_LIMIT_BYTES),
        name="attn_direct" if direct else "attn_online",
    )(qt, qe, kx, vt)


def _direct_softmax_is_safe(norm1_g, w_v, q_norm_g, k_norm_g):
    d_model = norm1_g.shape[0]
    qk_bits = LOG2E * math.sqrt(HEAD_DIM) * jnp.max(jnp.abs(q_norm_g)) * jnp.max(jnp.abs(k_norm_g))
    v_bound = (math.sqrt(d_model) * jnp.max(jnp.abs(norm1_g))
               * jnp.max(jnp.sqrt(jnp.sum(jnp.square(w_v.astype(F32)), axis=0))))
    return qk_bits + jnp.log2(jnp.maximum(v_bound, 1.0)) < DIRECT_SOFTMAX_MAX_BITS


def _lru_piece(rows, hprev, a_ref, u_ref, gg_ref, g_ref, rec_ref, after):
    a = a_ref[0, rows, :]
    u = u_ref[0, rows, :]
    pr, C = a.shape
    if after is not None:
        u = u + jnp.tile(_zero_after(after), (pr // SUBLANES, C // LANES))

    a, rin = _in_group(a)
    u, _ = _in_group(u)
    for sh in (1, 2, 4):
        ok = rin >= sh
        a_prev = jnp.where(ok, pltpu.roll(a, sh, axis=1), 1.0)
        u_prev = jnp.where(ok, pltpu.roll(u, sh, axis=1), 0.0)
        u = a * u_prev + u
        a = a * a_prev

    hs = []
    for gidx in range(pr // SUBLANES):
        hg = a[gidx] * hprev + u[gidx]
        hs.append(hg)
        hprev = jnp.broadcast_to(hg[SUBLANES - 1:SUBLANES, :], hg.shape)
    rec = jnp.concatenate(hs, axis=0) * gg_ref[0, rows, :].astype(F32)
    rec_ref[rows, :] = _rms(rec, g_ref[...]).astype(BF16)
    return hprev


def _mix_kernel(x_ref, att_ref, a_ref, u_ref, gg_ref, gl_ref, ga_ref, wo_ref, g2_ref, wg_ref,
                wu_ref, wd_ref, o_ref, act_ref, rec_ref, hc_ref, x1_ref, h2_ref,
                *, tf, pr, tiles_per_seq, n_tiles):
    g = pl.program_id(0)
    new, old = g % 2, 1 - g % 2

    @pl.when(g == 0)
    def _():
        for ref in (hc_ref, x1_ref, h2_ref):
            ref[...] = jnp.zeros_like(ref)

    h2 = h2_ref[...]
    first = (jnp.minimum(g, n_tiles - 1) % tiles_per_seq) == 0
    hprev = jnp.where(first, 0.0, hc_ref[...])

    tm = x_ref.shape[1]
    dff = wg_ref.shape[1]
    prev_act = None
    nchunk, npiece = dff // tf, tm // pr
    for c in range(nchunk):
        cols = slice(c * tf, (c + 1) * tf)
        act = jax.nn.silu(_dot(h2, wg_ref[:, cols])) * _dot(h2, wu_ref[:, cols])
        act_ref[:, cols] = act.astype(BF16)
        for p in range(npiece):
            if p * nchunk // npiece == c:
                hprev = _lru_piece(slice(p * pr, (p + 1) * pr), hprev, a_ref, u_ref, gg_ref,
                                   gl_ref, rec_ref, prev_act)
        prev_act = act
    hc_ref[...] = hprev

    aw = att_ref.shape[-1]
    attn = _rms(att_ref[0].astype(F32), ga_ref[...]).astype(BF16)
    x1 = x_ref[0] + _dot(attn, wo_ref[0:aw, :]) + _dot(rec_ref[...], wo_ref[aw:, :])
    x1_ref[new] = x1
    h2_ref[...] = _rms(x1, g2_ref[...]).astype(BF16)

    o_ref[0] = x1_ref[old] + _dot(act_ref[...], wd_ref[...])


def _mix_call(x, att, a, u, gg, gl, ga, wo, g2, wg, wu, wd, tm, tf, pr):
    B, S, D = x.shape
    aw = att.shape[-1]
    C = a.shape[-1]
    nS = S // tm
    N = B * nS
    const = lambda shape: pl.BlockSpec(shape, lambda g: (0,) * len(shape),
                                       pipeline_mode=pl.Buffered(1))

    def cur(w):
        return pl.BlockSpec((1, tm, w), lambda g: (jnp.minimum(g, N - 1) // nS,
                                                   jnp.minimum(g, N - 1) % nS, 0))

    def prev(w):
        return pl.BlockSpec((1, tm, w), lambda g: (jnp.maximum(g - 1, 0) // nS,
                                                   jnp.maximum(g - 1, 0) % nS, 0))

    return pl.pallas_call(
        functools.partial(_mix_kernel, tf=tf, pr=pr, tiles_per_seq=nS, n_tiles=N),
        grid=(N + 1,),
        in_specs=[cur(D), cur(aw), cur(C), cur(C), cur(C), const((1, C)), const((1, aw)),
                  const(wo.shape), const((1, D)), const(wg.shape), const(wu.shape),
                  const(wd.shape)],
        out_specs=prev(D),
        out_shape=jax.ShapeDtypeStruct((B, S, D), x.dtype),
        scratch_shapes=[pltpu.VMEM((tm, wg.shape[1]), BF16), pltpu.VMEM((tm, C), BF16),
                        pltpu.VMEM((SUBLANES, C), F32), pltpu.VMEM((2, tm, D), F32),
                        pltpu.VMEM((tm, D), BF16)],
        compiler_params=pltpu.CompilerParams(
            dimension_semantics=("arbitrary",),
            vmem_limit_bytes=VMEM_LIMIT_BYTES),
        name="mix",
    )(x, att, a, u, gg, gl, ga, wo, g2, wg, wu, wd)


def _block_diag(w):
    n, d, e = w.shape
    eye = jnp.eye(n, dtype=w.dtype)
    return jnp.einsum('nde,nm->ndme', w, eye).reshape(n * d, n * e)


def _bias_select():
    npair = ATT_WIDTH // HEAD_PAIR
    src = jnp.arange(LANES)
    piece, h = src // ATT_HEADS, src % ATT_HEADS
    slot = 3 * (h % 2) + piece
    valid = src < 3 * ATT_HEADS
    sel_k = ((jnp.arange(ATT_WIDTH)[None, :] == ((h // 2) * HEAD_PAIR + slot)[:, None])
             & valid[:, None])
    sel_q = ((jnp.arange(npair * BIAS_ROWS)[:, None] == ((h // 2) * BIAS_ROWS + 6 + slot)[None, :])
             & valid[None, :])
    in_pair = jnp.arange(ATT_WIDTH) % HEAD_PAIR
    kone = ((in_pair >= 6) & (in_pair < 12)).astype(F32).reshape(1, ATT_WIDTH)
    return sel_k.astype(BF16), sel_q.astype(BF16), kone


def _tiles(S):
    tp = min(1024, S)
    tm = min(512, S)
    tq = min(1024, S)
    td = min(256, S)
    tf = 256
    pr = min(16, S)
    return tp, tm, tq, td, tf, pr


def _layer(x, norm1_g, w_in, q_norm_g, k_norm_g, b_f, conv_w, conv_b, w_a, b_a, w_x, b_x,
           lam, attn_out_g, lru_out_g, w_out, norm2_g, w_gate, w_up, w_down):
    B, S, D = x.shape
    aw = ATT_WIDTH
    tp, tm, tq, td, tf, pr = _tiles(S)
    row = lambda a: a.reshape(1, -1).astype(F32)
    lane_pad = lambda a: jnp.pad(a, ((0, 0), (0, LANES - a.shape[1])))
    per_row = lambda g: jnp.broadcast_to(jnp.tile(g.astype(F32), ATT_HEADS)[:, None], (aw, LANES))

    wqkv = w_in[:, :3 * aw].astype(BF16)
    wf = lane_pad(jnp.tile(w_in[:, 3 * aw:3 * aw + ATT_HEADS], (1, 3))).astype(BF16)
    bf = lane_pad(jnp.tile(b_f.astype(F32).reshape(1, ATT_HEADS), (1, 3)))
    wlru = w_in[:, 3 * aw + ATT_HEADS:].astype(BF16)
    sel_k, sel_q, kone = _bias_select()

    qt, qe, kx, vt, a, u, gg = _proj_call(
        x, row(norm1_g), wqkv, wf, wlru, per_row(q_norm_g), per_row(k_norm_g), bf, sel_k, sel_q,
        kone, conv_w.astype(F32), row(conv_b), _block_diag(w_a).astype(BF16), row(b_a),
        _block_diag(w_x).astype(BF16), row(b_x), row(lam), tp)
    att = lax.cond(
        _direct_softmax_is_safe(norm1_g, w_in[:, 2 * aw:3 * aw], q_norm_g, k_norm_g),
        functools.partial(_attn_call, tq=tq, tk=tp, td=td, direct=True),
        functools.partial(_attn_call, tq=tq, tk=tp, td=td, direct=False),
        qt, qe, kx, vt)
    return _mix_call(x, att, a, u, gg, row(lru_out_g), row(attn_out_g), w_out.astype(BF16),
                     row(norm2_g), w_gate.astype(BF16), w_up.astype(BF16), w_down.astype(BF16),
                     tm, tf, pr)


def kernel(x, norm1_g, w_in, q_norm_g, k_norm_g, b_f, conv_w, conv_b, w_a, b_a, w_x, b_x, lam,
           attn_out_g, lru_out_g, w_out, norm2_g, w_gate, w_up, w_down):
    depth = norm1_g.shape[0]
    for l in range(depth):
        x = _layer(x, norm1_g[l], w_in[l], q_norm_g[l], k_norm_g[l], b_f[l], conv_w[l],
                   conv_b[l], w_a[l], b_a[l], w_x[l], b_x[l], lam[l], attn_out_g[l],
                   lru_out_g[l], w_out[l], norm2_g[l], w_gate[l], w_up[l], w_down[l])
    return x
```

```python
import functools
import math

import jax
import jax.numpy as jnp
from jax import lax
from jax.experimental import pallas as pl
from jax.experimental.pallas import tpu as pltpu

ATT_HEADS = 8
HEAD_DIM = 64
ATT_WIDTH = ATT_HEADS * HEAD_DIM
CONV_WIDTH = 4
LRU_C = 8.0
NORM_EPS = 1e-6
LOG2E = 1.4426950408889634

LANES = 128
SUBLANES = 8
HEAD_PAIR = 2 * HEAD_DIM
PIECES = 3
PAIR_SLOTS = 2 * PIECES
BIAS_ROWS = 16
GATE_BLOCK = 256
DIRECT_SOFTMAX_MAX_BITS = 100.0
VMEM_LIMIT_BYTES = 56 * 1024 * 1024

F32 = jnp.float32
BF16 = jnp.bfloat16


def _dot(a, b):
    return jnp.dot(a, b, preferred_element_type=F32)


def _dot_nt(a, b):
    return lax.dot_general(a, b, (((1,), (1,)), ((), ())), preferred_element_type=F32)


def _rms(xf, g):
    return xf * lax.rsqrt(jnp.mean(xf * xf, axis=-1, keepdims=True) + NORM_EPS) * g


def _split3(x):
    hi = x.astype(BF16)
    r = x - hi.astype(F32)
    mid = r.astype(BF16)
    lo = (r - mid.astype(F32)).astype(BF16)
    return hi, mid, lo


def _zero_after(v):
    bits = v[0:SUBLANES, 0:LANES].astype(jnp.int32)
    return lax.shift_right_logical(lax.shift_right_logical(bits, 16), 16).astype(F32)


def _gelu_tanh(x):
    c = math.sqrt(2.0 / math.pi)
    half = 0.5 * x
    return half + half * jnp.tanh(x * (c + (c * 0.044715) * (x * x)))


def _in_group(x):
    rows, C = x.shape
    x3 = x.reshape(rows // SUBLANES, SUBLANES, C)
    return x3, lax.broadcasted_iota(jnp.int32, x3.shape, 1)


def _head_norm_t(t, g_ref):
    tm = t.shape[1]
    outs = []
    for hh in range(ATT_HEADS):
        th = t[hh * HEAD_DIM:(hh + 1) * HEAD_DIM]
        outs.append(th * lax.rsqrt(jnp.mean(th * th, axis=0, keepdims=True) + NORM_EPS))
    return jnp.concatenate(outs, axis=0) * jnp.tile(g_ref[...], (1, tm // LANES))


def _proj_kernel(x_ref, g1_ref, wqkv_ref, wf_ref, wlru_ref, gq_ref, gk_ref, bf_ref,
                 selk_ref, selq_ref, kone_ref, cw_ref, cb_ref, wa_ref, ba_ref, wx_ref, bx_ref,
                 lam_ref,
                 qt_ref, qe_ref, kx_ref, vt_ref, a_ref, u_ref, gg_ref,
                 carry_ref, halo_ref, xc_ref, pa_ref, px_ref,
                 *, tiles_per_seq, n_tiles):
    g = pl.program_id(0)
    tm = x_ref.shape[1]
    C = a_ref.shape[-1]
    aw = ATT_WIDTH
    new, old = g % 2, 1 - g % 2

    @pl.when(g == 0)
    def _():
        for ref in (carry_ref, halo_ref, xc_ref, pa_ref, px_ref):
            ref[...] = jnp.zeros_like(ref)

    repeat = g == n_tiles
    first = (jnp.minimum(g, n_tiles - 1) % tiles_per_seq) == 0
    state = lambda ref: jnp.where(first, 0.0, jnp.where(repeat, ref[1], ref[0]))
    carry_in, halo = state(carry_ref), state(halo_ref)
    carry_ref[1] = carry_in
    halo_ref[1] = halo

    h = _rms(x_ref[0], g1_ref[...]).astype(BF16)

    lx = _dot(h, wlru_ref[:, 0:C])
    f_logit = _dot(h, wf_ref[...])
    q_raw = _dot(h, wqkv_ref[:, 0:aw])
    k_raw = _dot(h, wqkv_ref[:, aw:2 * aw])

    logf = jax.nn.log_sigmoid(f_logit + bf_ref[...])
    c3, rin = _in_group(logf)
    for sh in (1, 2, 4):
        c3 = c3 + jnp.where(rin >= sh, pltpu.roll(c3, sh, axis=1), 0.0)
    prev = carry_in
    groups = []
    for gidx in range(tm // SUBLANES):
        cg = c3[gidx] + prev
        groups.append(cg)
        prev = jnp.broadcast_to(cg[SUBLANES - 1:SUBLANES, :], cg.shape)
    carry_ref[0] = prev
    cum = jnp.concatenate(groups, axis=0)
    hi, mid, lo = _split3(cum * LOG2E)
    lane = lax.broadcasted_iota(jnp.int32, hi.shape, 1)
    pieces = jnp.where(lane < ATT_HEADS, hi, jnp.where(lane < 2 * ATT_HEADS, mid, lo))

    x3, rin = _in_group(lx)
    halo_ref[0] = lx[tm - SUBLANES:tm, :]
    xc3 = cb_ref[...] + x3 * cw_ref[CONV_WIDTH - 1:CONV_WIDTH, :]
    for k in range(1, CONV_WIDTH):
        rolled = pltpu.roll(x3, k, axis=1)
        before = jnp.concatenate([pltpu.roll(halo, k, axis=0)[None], rolled[:-1]], axis=0)
        tap = CONV_WIDTH - 1 - k
        xc3 = xc3 + jnp.where(rin >= k, rolled, before) * cw_ref[tap:tap + 1, :]
    xc = xc3.reshape(tm, C)
    xc_ref[new] = xc
    xcb = xc.astype(BF16)

    nparts = 8
    decay_rate = -LRU_C * jax.nn.softplus(-lam_ref[...])
    for part in range(nparts):
        rows = slice(part * (tm // nparts), (part + 1) * (tm // nparts))
        log_a = jax.nn.sigmoid(pa_ref[old, rows, :]) * decay_rate
        gi = jax.nn.sigmoid(px_ref[old, rows, :])
        a = jnp.exp(log_a)
        a_ref[0, rows, :] = a
        y = -jnp.tanh(log_a) * (1.0 + a * a)
        root = jnp.where(y > 0.0, y * lax.rsqrt(y), 0.0)
        u_ref[0, rows, :] = root * (gi * xc_ref[old, rows, :])

    v_raw = _dot(h, wqkv_ref[:, 2 * aw:3 * aw])
    ext_k = (_dot(pieces, selk_ref[...]) + kone_ref[...]).astype(BF16)
    qe_ref[0] = _dot_nt(selq_ref[...], pieces).astype(BF16)
    gg_ref[0] = _dot(h, wlru_ref[:, C:2 * C]).astype(BF16)
    for w_ref, b_ref, p_ref in ((wa_ref, ba_ref, pa_ref), (wx_ref, bx_ref, px_ref)):
        for n in range(C // GATE_BLOCK):
            blk = slice(n * GATE_BLOCK, (n + 1) * GATE_BLOCK)
            p_ref[new, :, blk] = _dot(xcb[:, blk], w_ref[blk, blk]) + b_ref[:, blk]

    q = _head_norm_t(q_raw.T, gq_ref) * (LOG2E / math.sqrt(HEAD_DIM))
    qt_ref[0] = q.astype(BF16)
    k = _head_norm_t(k_raw.T, gk_ref).T.astype(BF16)
    vt_ref[0, 0] = v_raw.T.astype(BF16).reshape(vt_ref.shape[2:])
    for p in range(aw // HEAD_PAIR):
        pair = slice(p * HEAD_PAIR, (p + 1) * HEAD_PAIR)
        kx_ref[0, :, 2 * p * HEAD_PAIR:(2 * p + 1) * HEAD_PAIR] = k[:, pair]
        kx_ref[0, :, (2 * p + 1) * HEAD_PAIR:(2 * p + 2) * HEAD_PAIR] = ext_k[:, pair]


def _proj_call(x, g1, wqkv, wf, wlru, gq, gk, bf, sel_k, sel_q, kone, cw, cb, wa, ba, wx, bx, lam,
               tm):
    B, S, D = x.shape
    aw = ATT_WIDTH
    npair = aw // HEAD_PAIR
    C = wlru.shape[1] // 2
    nS = S // tm
    N = B * nS
    const = lambda shape: pl.BlockSpec(shape, lambda g: (0,) * len(shape))
    cur = lambda g: (jnp.minimum(g, N - 1) // nS, jnp.minimum(g, N - 1) % nS)
    prev = lambda g: (jnp.maximum(g - 1, 0) // nS, jnp.maximum(g - 1, 0) % nS)
    tok = lambda w, tile: pl.BlockSpec((1, tm, w), lambda g: (*tile(g), 0))
    return pl.pallas_call(
        functools.partial(_proj_kernel, tiles_per_seq=nS, n_tiles=N),
        grid=(N + 1,),
        in_specs=[tok(D, cur), const((1, D)), const(wqkv.shape), const(wf.shape),
                  const(wlru.shape), const((aw, LANES)), const((aw, LANES)), const((1, LANES)),
                  const(sel_k.shape), const(sel_q.shape), const((1, aw)),
                  const(cw.shape), const((1, C)), const(wa.shape), const((1, C)),
                  const(wx.shape), const((1, C)), const((1, C))],
        out_specs=[pl.BlockSpec((1, aw, tm), lambda g: (cur(g)[0], 0, cur(g)[1])),
                   pl.BlockSpec((1, npair * BIAS_ROWS, tm), lambda g: (cur(g)[0], 0, cur(g)[1])),
                   tok(2 * aw, cur),
                   pl.BlockSpec((1, 1, npair, HEAD_PAIR, tm), lambda g: (*cur(g), 0, 0, 0)),
                   tok(C, prev), tok(C, prev), tok(C, cur)],
        out_shape=[jax.ShapeDtypeStruct((B, aw, S), BF16),
                   jax.ShapeDtypeStruct((B, npair * BIAS_ROWS, S), BF16),
                   jax.ShapeDtypeStruct((B, S, 2 * aw), BF16),
                   jax.ShapeDtypeStruct((B, S // tm, npair, HEAD_PAIR, tm), BF16),
                   jax.ShapeDtypeStruct((B, S, C), F32),
                   jax.ShapeDtypeStruct((B, S, C), F32),
                   jax.ShapeDtypeStruct((B, S, C), BF16)],
        scratch_shapes=[pltpu.VMEM((2, SUBLANES, LANES), F32), pltpu.VMEM((2, SUBLANES, C), F32)]
        + [pltpu.VMEM((2, tm, C), F32)] * 3,
        compiler_params=pltpu.CompilerParams(
            dimension_semantics=("arbitrary",),
            vmem_limit_bytes=VMEM_LIMIT_BYTES),
        name="proj",
    )(x, g1, wqkv, wf, wlru, gq, gk, bf, sel_k, sel_q, kone, cw, cb, wa, ba, wx, bx, lam)


def _stacked_queries(qt, qe, direct):
    tq = qt.shape[1]
    zeros = jnp.zeros((HEAD_DIM, tq), BF16)
    pad = jnp.zeros((HEAD_PAIR - BIAS_ROWS, tq), BF16)
    erow = lax.broadcasted_iota(jnp.int32, (BIAS_ROWS, tq), 0)

    def bias_rows(h):
        lo = PIECES * h
        rows = jnp.where((erow >= lo) & (erow < lo + PIECES), -1.0, 0.0).astype(BF16)
        if direct:
            rows = jnp.where((erow >= PAIR_SLOTS + lo) & (erow < PAIR_SLOTS + lo + PIECES), qe, rows)
        return rows

    return (jnp.concatenate([qt[0:HEAD_DIM], zeros, bias_rows(0), pad], axis=0),
            jnp.concatenate([zeros, qt[HEAD_DIM:HEAD_PAIR], bias_rows(1), pad], axis=0))


def _softmax_step(state, s, vb, head, direct):
    m, l, acc = state
    rows = slice(head * HEAD_DIM, (head + 1) * HEAD_DIM)
    if direct:
        p = jnp.exp2(s)
        l = l + jnp.sum(p, axis=0, keepdims=True)
        return None, l, acc + _dot(vb, p.astype(BF16))[rows]
    m_new = jnp.maximum(m, jnp.max(s, axis=0, keepdims=True))
    alpha = jnp.exp2(m - m_new)
    p = jnp.exp2(s - m_new)
    l = alpha * l + jnp.sum(p, axis=0, keepdims=True)
    return m_new, l, alpha * acc + _dot(vb, p.astype(BF16))[rows]


def _attn_tile(i, qt_ref, qe_ref, kx_ref, vt_ref, o_ref, *, tq, tk, td, direct):
    qs = _stacked_queries(qt_ref[0], qe_ref[0], direct)
    qst = jnp.concatenate(qs, axis=1)
    nk = tq // tk

    def full_scores(j):
        return _dot(kx_ref[0, j * tq:(j + 1) * tq, :], qst)

    def diag_scores():
        return [_dot(kx_ref[0, i * tq + c * td:i * tq + (c + 1) * td, :],
                     jnp.concatenate([qs[0][:, c * td:], qs[1][:, c * td:]], axis=1))
                for c in range(tq // td)]

    carry = tuple((None if direct else jnp.full((1, tq), -jnp.inf, F32), jnp.zeros((1, tq), F32),
                   jnp.zeros((HEAD_DIM, tq), F32)) for _ in range(2))
    ahead = full_scores(0) if i else diag_scores()
    for j in range(i):
        s2 = ahead
        ahead = full_scores(j + 1) if j + 1 < i else diag_scores()
        vb = jnp.concatenate([vt_ref[0, j * nk + n, 0] for n in range(nk)], axis=1)
        carry = tuple(_softmax_step(carry[h], s2[:, h * tq:(h + 1) * tq], vb, h, direct)
                      for h in range(2))

    krow = lax.broadcasted_iota(jnp.int32, (td, td), 0)
    qcol = lax.broadcasted_iota(jnp.int32, (td, td), 1)
    keep = krow <= qcol
    for c, s2 in enumerate(ahead):
        c0 = c * td
        w = tq - c0
        lo = c0 % tk
        vb = vt_ref[0, i * nk + c0 // tk, 0, :, lo:lo + td]
        new = []
        for h in range(2):
            s = s2[:, h * w:(h + 1) * w]
            sd = jnp.where(keep, s[:, 0:td], -jnp.inf)
            s = sd if w == td else jnp.concatenate([sd, s[:, td:]], axis=1)
            old = carry[h]
            part = _softmax_step(tuple(None if t is None else t[:, c0:] for t in old), s, vb, h,
                                 direct)
            if c0:
                part = tuple(None if t is None else jnp.concatenate([t[:, 0:c0], u], axis=1)
                             for t, u in zip(old, part))
            new.append(part)
        carry = tuple(new)

    (_, la, acca), (_, lb, accb) = carry
    ot = jnp.concatenate([acca / la, accb / lb], axis=0)
    o_ref[0] = ot.T.astype(o_ref.dtype)


def _attn_kernel(qt_ref, qe_ref, kx_ref, vt_ref, o_ref, *, n_tiles, **static):
    for i in range(n_tiles):
        pl.when(pl.program_id(2) == i)(
            functools.partial(_attn_tile, i, qt_ref, qe_ref, kx_ref, vt_ref, o_ref, **static))


def _attn_call(qt, qe, kx, vt, tq, tk, td, direct):
    B, aw, S = qt.shape
    npair = aw // HEAD_PAIR
    assert vt.shape == (B, S // tk, npair, HEAD_PAIR, tk)
    return pl.pallas_call(
        functools.partial(_attn_kernel, n_tiles=S // tq, tq=tq, tk=tk, td=td, direct=direct),
        grid=(B, npair, S // tq),
        in_specs=[pl.BlockSpec((1, HEAD_PAIR, tq), lambda b, p, i: (b, p, i)),
                  pl.BlockSpec((1, BIAS_ROWS, tq), lambda b, p, i: (b, p, i)),
                  pl.BlockSpec((1, S, 2 * HEAD_PAIR), lambda b, p, i: (b, 0, p)),
                  pl.BlockSpec((1, S // tk, 1, HEAD_PAIR, tk), lambda b, p, i: (b, 0, p, 0, 0))],
        out_specs=pl.BlockSpec((1, tq, HEAD_PAIR), lambda b, p, i: (b, i, p)),
        out_shape=jax.ShapeDtypeStruct((B, S, aw), BF16),
        compiler_params=pltpu.CompilerParams(
            dimension_semantics=("arbitrary", "arbitrary", "arbitrary"),
            vmem_limit_bytes=VMEM_LIMIT_BYTES),
        name="attn_direct" if direct else "attn_online",
    )(qt, qe, kx, vt)


def _direct_softmax_is_safe(norm1_g, w_v, q_norm_g, k_norm_g):
    d_model = norm1_g.shape[0]
    qk_bits = LOG2E * math.sqrt(HEAD_DIM) * jnp.max(jnp.abs(q_norm_g)) * jnp.max(jnp.abs(k_norm_g))
    v_bound = (math.sqrt(d_model) * jnp.max(jnp.abs(norm1_g))
               * jnp.max(jnp.sqrt(jnp.sum(jnp.square(w_v.astype(F32)), axis=0))))
    return qk_bits + jnp.log2(jnp.maximum(v_bound, 1.0)) < DIRECT_SOFTMAX_MAX_BITS


def _lru_piece(rows, hprev, a_ref, u_ref, gg_ref, g_ref, rec_ref, after):
    a = a_ref[0, rows, :]
    u = u_ref[0, rows, :]
    pr, C = a.shape
    if after is not None:
        u = u + jnp.tile(_zero_after(after), (pr // SUBLANES, C // LANES))

    a, rin = _in_group(a)
    u, _ = _in_group(u)
    for sh in (1, 2, 4):
        ok = rin >= sh
        a_prev = jnp.where(ok, pltpu.roll(a, sh, axis=1), 1.0)
        u_prev = jnp.where(ok, pltpu.roll(u, sh, axis=1), 0.0)
        u = a * u_prev + u
        a = a * a_prev

    hs = []
    for gidx in range(pr // SUBLANES):
        hg = a[gidx] * hprev + u[gidx]
        hs.append(hg)
        hprev = jnp.broadcast_to(hg[SUBLANES - 1:SUBLANES, :], hg.shape)
    rec = jnp.concatenate(hs, axis=0) * _gelu_tanh(gg_ref[0, rows, :].astype(F32))
    rec_ref[rows, :] = _rms(rec, g_ref[...]).astype(BF16)
    return hprev


def _mix_kernel(x_ref, att_ref, a_ref, u_ref, gg_ref, gl_ref, ga_ref, wo_ref, g2_ref, wg_ref,
                wu_ref, wd_ref, o_ref, act_ref, rec_ref, hc_ref, x1_ref, h2_ref,
                *, tf, pr, tiles_per_seq, n_tiles):
    g = pl.program_id(0)
    new, old = g % 2, 1 - g % 2

    @pl.when(g == 0)
    def _():
        for ref in (hc_ref, x1_ref, h2_ref):
            ref[...] = jnp.zeros_like(ref)

    h2 = h2_ref[...]
    first = (jnp.minimum(g, n_tiles - 1) % tiles_per_seq) == 0
    hprev = jnp.where(first, 0.0, hc_ref[...])

    tm = x_ref.shape[1]
    dff = wg_ref.shape[1]
    prev_act = None
    nchunk, npiece = dff // tf, tm // pr
    for c in range(nchunk):
        cols = slice(c * tf, (c + 1) * tf)
        act = jax.nn.silu(_dot(h2, wg_ref[:, cols])) * _dot(h2, wu_ref[:, cols])
        act_ref[:, cols] = act.astype(BF16)
        for p in range(npiece):
            if p * nchunk // npiece == c:
                hprev = _lru_piece(slice(p * pr, (p + 1) * pr), hprev, a_ref, u_ref, gg_ref,
                                   gl_ref, rec_ref, prev_act)
        prev_act = act
    hc_ref[...] = hprev

    aw = att_ref.shape[-1]
    attn = _rms(att_ref[0].astype(F32), ga_ref[...]).astype(BF16)
    x1 = x_ref[0] + _dot(attn, wo_ref[0:aw, :]) + _dot(rec_ref[...], wo_ref[aw:, :])
    x1_ref[new] = x1
    h2_ref[...] = _rms(x1, g2_ref[...]).astype(BF16)

    o_ref[0] = x1_ref[old] + _dot(act_ref[...], wd_ref[...])


def _mix_call(x, att, a, u, gg, gl, ga, wo, g2, wg, wu, wd, tm, tf, pr):
    B, S, D = x.shape
    aw = att.shape[-1]
    C = a.shape[-1]
    nS = S // tm
    N = B * nS
    const = lambda shape: pl.BlockSpec(shape, lambda g: (0,) * len(shape),
                                       pipeline_mode=pl.Buffered(1))

    def cur(w):
        return pl.BlockSpec((1, tm, w), lambda g: (jnp.minimum(g, N - 1) // nS,
                                                   jnp.minimum(g, N - 1) % nS, 0))

    def prev(w):
        return pl.BlockSpec((1, tm, w), lambda g: (jnp.maximum(g - 1, 0) // nS,
                                                   jnp.maximum(g - 1, 0) % nS, 0))

    return pl.pallas_call(
        functools.partial(_mix_kernel, tf=tf, pr=pr, tiles_per_seq=nS, n_tiles=N),
        grid=(N + 1,),
        in_specs=[cur(D), cur(aw), cur(C), cur(C), cur(C), const((1, C)), const((1, aw)),
                  const(wo.shape), const((1, D)), const(wg.shape), const(wu.shape),
                  const(wd.shape)],
        out_specs=prev(D),
        out_shape=jax.ShapeDtypeStruct((B, S, D), x.dtype),
        scratch_shapes=[pltpu.VMEM((tm, wg.shape[1]), BF16), pltpu.VMEM((tm, C), BF16),
                        pltpu.VMEM((SUBLANES, C), F32), pltpu.VMEM((2, tm, D), F32),
                        pltpu.VMEM((tm, D), BF16)],
        compiler_params=pltpu.CompilerParams(
            dimension_semantics=("arbitrary",),
            vmem_limit_bytes=VMEM_LIMIT_BYTES),
        name="mix",
    )(x, att, a, u, gg, gl, ga, wo, g2, wg, wu, wd)


def _block_diag(w):
    n, d, e = w.shape
    eye = jnp.eye(n, dtype=w.dtype)
    return jnp.einsum('nde,nm->ndme', w, eye).reshape(n * d, n * e)


def _bias_select():
    npair = ATT_WIDTH // HEAD_PAIR
    src = jnp.arange(LANES)
    piece, h = src // ATT_HEADS, src % ATT_HEADS
    slot = PIECES * (h % 2) + piece
    valid = src < PIECES * ATT_HEADS
    sel_k = ((jnp.arange(ATT_WIDTH)[None, :] == ((h // 2) * HEAD_PAIR + slot)[:, None])
             & valid[:, None])
    sel_q = ((jnp.arange(npair * BIAS_ROWS)[:, None] == ((h // 2) * BIAS_ROWS + PAIR_SLOTS + slot)[None, :])
             & valid[None, :])
    in_pair = jnp.arange(ATT_WIDTH) % HEAD_PAIR
    kone = ((in_pair >= PAIR_SLOTS) & (in_pair < 2 * PAIR_SLOTS)).astype(F32).reshape(1, ATT_WIDTH)
    return sel_k.astype(BF16), sel_q.astype(BF16), kone


def _tiles(S):
    tp = min(1024, S)
    tm = min(512, S)
    tq = min(1024, S)
    td = min(256, S)
    tf = 256
    pr = min(16, S)
    return tp, tm, tq, td, tf, pr


def _layer(x, norm1_g, w_in, q_norm_g, k_norm_g, b_f, conv_w, conv_b, w_a, b_a, w_x, b_x,
           lam, attn_out_g, lru_out_g, w_out, norm2_g, w_gate, w_up, w_down):
    B, S, D = x.shape
    aw = ATT_WIDTH
    tp, tm, tq, td, tf, pr = _tiles(S)
    row = lambda a: a.reshape(1, -1).astype(F32)
    lane_pad = lambda a: jnp.pad(a, ((0, 0), (0, LANES - a.shape[1])))
    per_row = lambda g: jnp.broadcast_to(jnp.tile(g.astype(F32), ATT_HEADS)[:, None], (aw, LANES))

    wqkv = w_in[:, :3 * aw].astype(BF16)
    wf = lane_pad(jnp.tile(w_in[:, 3 * aw:3 * aw + ATT_HEADS], (1, PIECES))).astype(BF16)
    bf = lane_pad(jnp.tile(b_f.astype(F32).reshape(1, ATT_HEADS), (1, PIECES)))
    wlru = w_in[:, 3 * aw + ATT_HEADS:].astype(BF16)
    sel_k, sel_q, kone = _bias_select()

    qt, qe, kx, vt, a, u, gg = _proj_call(
        x, row(norm1_g), wqkv, wf, wlru, per_row(q_norm_g), per_row(k_norm_g), bf, sel_k, sel_q,
        kone, conv_w.astype(F32), row(conv_b), _block_diag(w_a).astype(BF16), row(b_a),
        _block_diag(w_x).astype(BF16), row(b_x), row(lam), tp)
    att = lax.cond(
        _direct_softmax_is_safe(norm1_g, w_in[:, 2 * aw:3 * aw], q_norm_g, k_norm_g),
        functools.partial(_attn_call, tq=tq, tk=tp, td=td, direct=True),
        functools.partial(_attn_call, tq=tq, tk=tp, td=td, direct=False),
        qt, qe, kx, vt)
    return _mix_call(x, att, a, u, gg, row(lru_out_g), row(attn_out_g), w_out.astype(BF16),
                     row(norm2_g), w_gate.astype(BF16), w_up.astype(BF16), w_down.astype(BF16),
                     tm, tf, pr)


def kernel(x, norm1_g, w_in, q_norm_g, k_norm_g, b_f, conv_w, conv_b, w_a, b_a, w_x, b_x, lam,
           attn_out_g, lru_out_g, w_out, norm2_g, w_gate, w_up, w_down):
    depth = norm1_g.shape[0]
    for l in range(depth):
        x = _layer(x, norm1_g[l], w_in[l], q_norm_g[l], k_norm_g[l], b_f[l], conv_w[l],
                   conv_b[l], w_a[l], b_a[l], w_x[l], b_x[l], lam[l], attn_out_g[l],
                   lru_out_g[l], w_out[l], norm2_g[l], w_gate[l], w_up[l], w_down[l])
    return x
```

```python
import functools
import math

import jax
import jax.numpy as jnp
from jax import lax
from jax.experimental import pallas as pl
from jax.experimental.pallas import tpu as pltpu

ATT_HEADS = 8
HEAD_DIM = 64
ATT_WIDTH = ATT_HEADS * HEAD_DIM
CONV_WIDTH = 4
LRU_C = 8.0
NORM_EPS = 1e-6
LOG2E = 1.4426950408889634

LANES = 128
SUBLANES = 8
HEAD_PAIR = 2 * HEAD_DIM
PIECES = 3
PAIR_SLOTS = 2 * PIECES
BIAS_ROWS = 16
GATE_BLOCK = 256
PAIRS_PER_STEP = 2
DIRECT_SOFTMAX_MAX_BITS = 100.0
VMEM_LIMIT_BYTES = 56 * 1024 * 1024

F32 = jnp.float32
BF16 = jnp.bfloat16


def _dot(a, b):
    return jnp.dot(a, b, preferred_element_type=F32)


def _dot_nt(a, b):
    return lax.dot_general(a, b, (((1,), (1,)), ((), ())), preferred_element_type=F32)


def _rms(xf, g):
    return xf * lax.rsqrt(jnp.mean(xf * xf, axis=-1, keepdims=True) + NORM_EPS) * g


def _split3(x):
    hi = x.astype(BF16)
    r = x - hi.astype(F32)
    mid = r.astype(BF16)
    lo = (r - mid.astype(F32)).astype(BF16)
    return hi, mid, lo


def _zero_after(v):
    bits = v[0:SUBLANES, 0:LANES].astype(jnp.int32)
    return lax.shift_right_logical(lax.shift_right_logical(bits, 16), 16).astype(F32)


def _gelu_tanh(x):
    c = math.sqrt(2.0 / math.pi)
    half = 0.5 * x
    return half + half * jnp.tanh(x * (c + (c * 0.044715) * (x * x)))


def _in_group(x):
    rows, C = x.shape
    x3 = x.reshape(rows // SUBLANES, SUBLANES, C)
    return x3, lax.broadcasted_iota(jnp.int32, x3.shape, 1)


def _head_norm_t(t, g_ref):
    tm = t.shape[1]
    outs = []
    for hh in range(ATT_HEADS):
        th = t[hh * HEAD_DIM:(hh + 1) * HEAD_DIM]
        outs.append(th * lax.rsqrt(jnp.mean(th * th, axis=0, keepdims=True) + NORM_EPS))
    return jnp.concatenate(outs, axis=0) * jnp.tile(g_ref[...], (1, tm // LANES))


def _proj_kernel(x_ref, g1_ref, wqkv_ref, wf_ref, wlru_ref, gq_ref, gk_ref, bf_ref,
                 selk_ref, selq_ref, kone_ref, cw_ref, cb_ref, wa_ref, ba_ref, wx_ref, bx_ref,
                 lam_ref,
                 qt_ref, qe_ref, kx_ref, vt_ref, a_ref, u_ref, gg_ref,
                 carry_ref, halo_ref, xc_ref, pa_ref, px_ref,
                 *, tiles_per_seq, n_tiles):
    g = pl.program_id(0)
    tm = x_ref.shape[1]
    C = a_ref.shape[-1]
    aw = ATT_WIDTH
    new, old = g % 2, 1 - g % 2

    @pl.when(g == 0)
    def _():
        for ref in (carry_ref, halo_ref, xc_ref, pa_ref, px_ref):
            ref[...] = jnp.zeros_like(ref)

    repeat = g == n_tiles
    first = (jnp.minimum(g, n_tiles - 1) % tiles_per_seq) == 0
    state = lambda ref: jnp.where(first, 0.0, jnp.where(repeat, ref[1], ref[0]))
    carry_in, halo = state(carry_ref), state(halo_ref)
    carry_ref[1] = carry_in
    halo_ref[1] = halo

    h = _rms(x_ref[0], g1_ref[...]).astype(BF16)

    lx = _dot(h, wlru_ref[:, 0:C])
    f_logit = _dot(h, wf_ref[...])
    q_raw = _dot(h, wqkv_ref[:, 0:aw])
    k_raw = _dot(h, wqkv_ref[:, aw:2 * aw])

    logf = jax.nn.log_sigmoid(f_logit + bf_ref[...])
    c3, rin = _in_group(logf)
    for sh in (1, 2, 4):
        c3 = c3 + jnp.where(rin >= sh, pltpu.roll(c3, sh, axis=1), 0.0)
    prev = carry_in
    groups = []
    for gidx in range(tm // SUBLANES):
        cg = c3[gidx] + prev
        groups.append(cg)
        prev = jnp.broadcast_to(cg[SUBLANES - 1:SUBLANES, :], cg.shape)
    carry_ref[0] = prev
    cum = jnp.concatenate(groups, axis=0)
    hi, mid, lo = _split3(cum * LOG2E)
    lane = lax.broadcasted_iota(jnp.int32, hi.shape, 1)
    pieces = jnp.where(lane < ATT_HEADS, hi, jnp.where(lane < 2 * ATT_HEADS, mid, lo))

    x3, rin = _in_group(lx)
    halo_ref[0] = lx[tm - SUBLANES:tm, :]
    xc3 = cb_ref[...] + x3 * cw_ref[CONV_WIDTH - 1:CONV_WIDTH, :]
    for k in range(1, CONV_WIDTH):
        rolled = pltpu.roll(x3, k, axis=1)
        before = jnp.concatenate([pltpu.roll(halo, k, axis=0)[None], rolled[:-1]], axis=0)
        tap = CONV_WIDTH - 1 - k
        xc3 = xc3 + jnp.where(rin >= k, rolled, before) * cw_ref[tap:tap + 1, :]
    xc = xc3.reshape(tm, C)
    xc_ref[new] = xc
    xcb = xc.astype(BF16)

    nparts = 8
    decay_rate = -LRU_C * jax.nn.softplus(-lam_ref[...])
    for part in range(nparts):
        rows = slice(part * (tm // nparts), (part + 1) * (tm // nparts))
        log_a = jax.nn.sigmoid(pa_ref[old, rows, :]) * decay_rate
        gi = jax.nn.sigmoid(px_ref[old, rows, :])
        a = jnp.exp(log_a)
        a_ref[0, rows, :] = a
        y = -jnp.tanh(log_a) * (1.0 + a * a)
        root = jnp.where(y > 0.0, y * lax.rsqrt(y), 0.0)
        u_ref[0, rows, :] = root * (gi * xc_ref[old, rows, :])

    v_raw = _dot(h, wqkv_ref[:, 2 * aw:3 * aw])
    ext_k = (_dot(pieces, selk_ref[...]) + kone_ref[...]).astype(BF16)
    qe_ref[0] = _dot_nt(selq_ref[...], pieces).astype(BF16)
    gg_ref[0] = _dot(h, wlru_ref[:, C:2 * C]).astype(BF16)
    for w_ref, b_ref, p_ref in ((wa_ref, ba_ref, pa_ref), (wx_ref, bx_ref, px_ref)):
        for n in range(C // GATE_BLOCK):
            blk = slice(n * GATE_BLOCK, (n + 1) * GATE_BLOCK)
            p_ref[new, :, blk] = _dot(xcb[:, blk], w_ref[blk, blk]) + b_ref[:, blk]

    q = _head_norm_t(q_raw.T, gq_ref) * (LOG2E / math.sqrt(HEAD_DIM))
    qt_ref[0] = q.astype(BF16)
    k = _head_norm_t(k_raw.T, gk_ref).T.astype(BF16)
    vt_ref[0, 0] = v_raw.T.astype(BF16).reshape(vt_ref.shape[2:])
    for p in range(aw // HEAD_PAIR):
        pair = slice(p * HEAD_PAIR, (p + 1) * HEAD_PAIR)
        kx_ref[0, :, 2 * p * HEAD_PAIR:(2 * p + 1) * HEAD_PAIR] = k[:, pair]
        kx_ref[0, :, (2 * p + 1) * HEAD_PAIR:(2 * p + 2) * HEAD_PAIR] = ext_k[:, pair]


def _proj_call(x, g1, wqkv, wf, wlru, gq, gk, bf, sel_k, sel_q, kone, cw, cb, wa, ba, wx, bx, lam,
               tm):
    B, S, D = x.shape
    aw = ATT_WIDTH
    npair = aw // HEAD_PAIR
    C = wlru.shape[1] // 2
    nS = S // tm
    N = B * nS
    const = lambda shape: pl.BlockSpec(shape, lambda g: (0,) * len(shape))
    cur = lambda g: (jnp.minimum(g, N - 1) // nS, jnp.minimum(g, N - 1) % nS)
    prev = lambda g: (jnp.maximum(g - 1, 0) // nS, jnp.maximum(g - 1, 0) % nS)
    tok = lambda w, tile: pl.BlockSpec((1, tm, w), lambda g: (*tile(g), 0))
    return pl.pallas_call(
        functools.partial(_proj_kernel, tiles_per_seq=nS, n_tiles=N),
        grid=(N + 1,),
        in_specs=[tok(D, cur), const((1, D)), const(wqkv.shape), const(wf.shape),
                  const(wlru.shape), const((aw, LANES)), const((aw, LANES)), const((1, LANES)),
                  const(sel_k.shape), const(sel_q.shape), const((1, aw)),
                  const(cw.shape), const((1, C)), const(wa.shape), const((1, C)),
                  const(wx.shape), const((1, C)), const((1, C))],
        out_specs=[pl.BlockSpec((1, aw, tm), lambda g: (cur(g)[0], 0, cur(g)[1])),
                   pl.BlockSpec((1, npair * BIAS_ROWS, tm), lambda g: (cur(g)[0], 0, cur(g)[1])),
                   tok(2 * aw, cur),
                   pl.BlockSpec((1, 1, npair, HEAD_PAIR, tm), lambda g: (*cur(g), 0, 0, 0)),
                   tok(C, prev), tok(C, prev), tok(C, cur)],
        out_shape=[jax.ShapeDtypeStruct((B, aw, S), BF16),
                   jax.ShapeDtypeStruct((B, npair * BIAS_ROWS, S), BF16),
                   jax.ShapeDtypeStruct((B, S, 2 * aw), BF16),
                   jax.ShapeDtypeStruct((B, S // tm, npair, HEAD_PAIR, tm), BF16),
                   jax.ShapeDtypeStruct((B, S, C), F32),
                   jax.ShapeDtypeStruct((B, S, C), F32),
                   jax.ShapeDtypeStruct((B, S, C), BF16)],
        scratch_shapes=[pltpu.VMEM((2, SUBLANES, LANES), F32), pltpu.VMEM((2, SUBLANES, C), F32)]
        + [pltpu.VMEM((2, tm, C), F32)] * 3,
        compiler_params=pltpu.CompilerParams(
            dimension_semantics=("arbitrary",),
            vmem_limit_bytes=VMEM_LIMIT_BYTES),
        name="proj",
    )(x, g1, wqkv, wf, wlru, gq, gk, bf, sel_k, sel_q, kone, cw, cb, wa, ba, wx, bx, lam)


def _stacked_queries(qt, qe, direct):
    tq = qt.shape[1]
    zeros = jnp.zeros((HEAD_DIM, tq), BF16)
    pad = jnp.zeros((HEAD_PAIR - BIAS_ROWS, tq), BF16)
    erow = lax.broadcasted_iota(jnp.int32, (BIAS_ROWS, tq), 0)

    def bias_rows(h):
        lo = PIECES * h
        rows = jnp.where((erow >= lo) & (erow < lo + PIECES), -1.0, 0.0).astype(BF16)
        if direct:
            rows = jnp.where((erow >= PAIR_SLOTS + lo) & (erow < PAIR_SLOTS + lo + PIECES), qe, rows)
        return rows

    return (jnp.concatenate([qt[0:HEAD_DIM], zeros, bias_rows(0), pad], axis=0),
            jnp.concatenate([zeros, qt[HEAD_DIM:HEAD_PAIR], bias_rows(1), pad], axis=0))


def _softmax_step(state, s, vb, head, direct):
    m, l, acc = state
    rows = slice(head * HEAD_DIM, (head + 1) * HEAD_DIM)
    if direct:
        p = jnp.exp2(s)
        l = l + jnp.sum(p, axis=0, keepdims=True)
        return None, l, acc + _dot(vb, p.astype(BF16))[rows]
    m_new = jnp.maximum(m, jnp.max(s, axis=0, keepdims=True))
    alpha = jnp.exp2(m - m_new)
    p = jnp.exp2(s - m_new)
    l = alpha * l + jnp.sum(p, axis=0, keepdims=True)
    return m_new, l, alpha * acc + _dot(vb, p.astype(BF16))[rows]


def _attn_tile(i, qt_ref, qe_ref, kx_ref, vt_ref, o_ref, *, tq, tk, td, direct):
    qs = _stacked_queries(qt_ref[0], qe_ref[0], direct)
    qst = jnp.concatenate(qs, axis=1)
    nk = tq // tk

    def full_scores(j):
        return _dot(kx_ref[0, j * tq:(j + 1) * tq, :], qst)

    def diag_scores():
        return [_dot(kx_ref[0, i * tq + c * td:i * tq + (c + 1) * td, :],
                     jnp.concatenate([qs[0][:, c * td:], qs[1][:, c * td:]], axis=1))
                for c in range(tq // td)]

    carry = tuple((None if direct else jnp.full((1, tq), -jnp.inf, F32), jnp.zeros((1, tq), F32),
                   jnp.zeros((HEAD_DIM, tq), F32)) for _ in range(2))
    ahead = full_scores(0) if i else diag_scores()
    for j in range(i):
        s2 = ahead
        ahead = full_scores(j + 1) if j + 1 < i else diag_scores()
        vb = jnp.concatenate([vt_ref[0, j * nk + n, 0] for n in range(nk)], axis=1)
        carry = tuple(_softmax_step(carry[h], s2[:, h * tq:(h + 1) * tq], vb, h, direct)
                      for h in range(2))

    krow = lax.broadcasted_iota(jnp.int32, (td, td), 0)
    qcol = lax.broadcasted_iota(jnp.int32, (td, td), 1)
    keep = krow <= qcol
    for c, s2 in enumerate(ahead):
        c0 = c * td
        w = tq - c0
        lo = c0 % tk
        vb = vt_ref[0, i * nk + c0 // tk, 0, :, lo:lo + td]
        new = []
        for h in range(2):
            s = s2[:, h * w:(h + 1) * w]
            sd = jnp.where(keep, s[:, 0:td], -jnp.inf)
            s = sd if w == td else jnp.concatenate([sd, s[:, td:]], axis=1)
            old = carry[h]
            part = _softmax_step(tuple(None if t is None else t[:, c0:] for t in old), s, vb, h,
                                 direct)
            if c0:
                part = tuple(None if t is None else jnp.concatenate([t[:, 0:c0], u], axis=1)
                             for t, u in zip(old, part))
            new.append(part)
        carry = tuple(new)

    (_, la, acca), (_, lb, accb) = carry
    ot = jnp.concatenate([acca / la, accb / lb], axis=0)
    o_ref[0] = ot.T.astype(o_ref.dtype)


def _attn_kernel(qt_ref, qe_ref, kx_ref, vt_ref, o_ref, *, n_tiles, **static):
    def tile(i):
        for pp in range(PAIRS_PER_STEP):
            rows = lambda n: pl.ds(pp * n, n)
            _attn_tile(i, qt_ref.at[:, rows(HEAD_PAIR), :], qe_ref.at[:, rows(BIAS_ROWS), :],
                       kx_ref.at[:, :, rows(2 * HEAD_PAIR)], vt_ref.at[:, :, rows(1)],
                       o_ref.at[:, :, rows(HEAD_PAIR)], **static)

    for i in range(n_tiles):
        pl.when(pl.program_id(2) == i)(functools.partial(tile, i))


def _attn_call(qt, qe, kx, vt, tq, tk, td, direct):
    B, aw, S = qt.shape
    npair = aw // HEAD_PAIR
    pps = PAIRS_PER_STEP
    assert vt.shape == (B, S // tk, npair, HEAD_PAIR, tk) and npair % pps == 0
    return pl.pallas_call(
        functools.partial(_attn_kernel, n_tiles=S // tq, tq=tq, tk=tk, td=td, direct=direct),
        grid=(B, npair // pps, S // tq),
        in_specs=[pl.BlockSpec((1, pps * HEAD_PAIR, tq), lambda b, p, i: (b, p, i)),
                  pl.BlockSpec((1, pps * BIAS_ROWS, tq), lambda b, p, i: (b, p, i)),
                  pl.BlockSpec((1, S, pps * 2 * HEAD_PAIR), lambda b, p, i: (b, 0, p)),
                  pl.BlockSpec((1, S // tk, pps, HEAD_PAIR, tk), lambda b, p, i: (b, 0, p, 0, 0))],
        out_specs=pl.BlockSpec((1, tq, pps * HEAD_PAIR), lambda b, p, i: (b, i, p)),
        out_shape=jax.ShapeDtypeStruct((B, S, aw), BF16),
        compiler_params=pltpu.CompilerParams(
            dimension_semantics=("arbitrary", "arbitrary", "arbitrary"),
            vmem_limit_bytes=VMEM_LIMIT_BYTES),
        name="attn_direct" if direct else "attn_online",
    )(qt, qe, kx, vt)


def _direct_softmax_is_safe(norm1_g, w_v, q_norm_g, k_norm_g):
    d_model = norm1_g.shape[0]
    qk_bits = LOG2E * math.sqrt(HEAD_DIM) * jnp.max(jnp.abs(q_norm_g)) * jnp.max(jnp.abs(k_norm_g))
    v_bound = (math.sqrt(d_model) * jnp.max(jnp.abs(norm1_g))
               * jnp.max(jnp.sqrt(jnp.sum(jnp.square(w_v.astype(F32)), axis=0))))
    return qk_bits + jnp.log2(jnp.maximum(v_bound, 1.0)) < DIRECT_SOFTMAX_MAX_BITS


def _lru_piece(rows, hprev, a_ref, u_ref, gg_ref, g_ref, rec_ref, after):
    a = a_ref[0, rows, :]
    u = u_ref[0, rows, :]
    pr, C = a.shape
    if after is not None:
        u = u + jnp.tile(_zero_after(after), (pr // SUBLANES, C // LANES))

    a, rin = _in_group(a)
    u, _ = _in_group(u)
    for sh in (1, 2, 4):
        ok = rin >= sh
        a_prev = jnp.where(ok, pltpu.roll(a, sh, axis=1), 1.0)
        u_prev = jnp.where(ok, pltpu.roll(u, sh, axis=1), 0.0)
        u = a * u_prev + u
        a = a * a_prev

    hs = []
    for gidx in range(pr // SUBLANES):
        hg = a[gidx] * hprev + u[gidx]
        hs.append(hg)
        hprev = jnp.broadcast_to(hg[SUBLANES - 1:SUBLANES, :], hg.shape)
    rec = jnp.concatenate(hs, axis=0) * _gelu_tanh(gg_ref[0, rows, :].astype(F32))
    rec_ref[rows, :] = _rms(rec, g_ref[...]).astype(BF16)
    return hprev


def _mix_kernel(x_ref, att_ref, a_ref, u_ref, gg_ref, gl_ref, ga_ref, wo_ref, g2_ref, wg_ref,
                wu_ref, wd_ref, o_ref, act_ref, rec_ref, hc_ref, x1_ref, h2_ref,
                *, tf, pr, tiles_per_seq, n_tiles):
    g = pl.program_id(0)
    new, old = g % 2, 1 - g % 2

    @pl.when(g == 0)
    def _():
        for ref in (hc_ref, x1_ref, h2_ref):
            ref[...] = jnp.zeros_like(ref)

    h2 = h2_ref[...]
    first = (jnp.minimum(g, n_tiles - 1) % tiles_per_seq) == 0
    hprev = jnp.where(first, 0.0, hc_ref[...])

    tm = x_ref.shape[1]
    dff = wg_ref.shape[1]
    prev_act = None
    nchunk, npiece = dff // tf, tm // pr
    for c in range(nchunk):
        cols = slice(c * tf, (c + 1) * tf)
        act = jax.nn.silu(_dot(h2, wg_ref[:, cols])) * _dot(h2, wu_ref[:, cols])
        act_ref[:, cols] = act.astype(BF16)
        for p in range(npiece):
            if p * nchunk // npiece == c:
                hprev = _lru_piece(slice(p * pr, (p + 1) * pr), hprev, a_ref, u_ref, gg_ref,
                                   gl_ref, rec_ref, prev_act)
        prev_act = act
    hc_ref[...] = hprev

    aw = att_ref.shape[-1]
    attn = _rms(att_ref[0].astype(F32), ga_ref[...]).astype(BF16)
    x1 = x_ref[0] + _dot(attn, wo_ref[0:aw, :]) + _dot(rec_ref[...], wo_ref[aw:, :])
    x1_ref[new] = x1
    h2_ref[...] = _rms(x1, g2_ref[...]).astype(BF16)

    o_ref[0] = x1_ref[old] + _dot(act_ref[...], wd_ref[...])


def _mix_call(x, att, a, u, gg, gl, ga, wo, g2, wg, wu, wd, tm, tf, pr):
    B, S, D = x.shape
    aw = att.shape[-1]
    C = a.shape[-1]
    nS = S // tm
    N = B * nS
    const = lambda shape: pl.BlockSpec(shape, lambda g: (0,) * len(shape),
                                       pipeline_mode=pl.Buffered(1))

    def cur(w):
        return pl.BlockSpec((1, tm, w), lambda g: (jnp.minimum(g, N - 1) // nS,
                                                   jnp.minimum(g, N - 1) % nS, 0))

    def prev(w):
        return pl.BlockSpec((1, tm, w), lambda g: (jnp.maximum(g - 1, 0) // nS,
                                                   jnp.maximum(g - 1, 0) % nS, 0))

    return pl.pallas_call(
        functools.partial(_mix_kernel, tf=tf, pr=pr, tiles_per_seq=nS, n_tiles=N),
        grid=(N + 1,),
        in_specs=[cur(D), cur(aw), cur(C), cur(C), cur(C), const((1, C)), const((1, aw)),
                  const(wo.shape), const((1, D)), const(wg.shape), const(wu.shape),
                  const(wd.shape)],
        out_specs=prev(D),
        out_shape=jax.ShapeDtypeStruct((B, S, D), x.dtype),
        scratch_shapes=[pltpu.VMEM((tm, wg.shape[1]), BF16), pltpu.VMEM((tm, C), BF16),
                        pltpu.VMEM((SUBLANES, C), F32), pltpu.VMEM((2, tm, D), F32),
                        pltpu.VMEM((tm, D), BF16)],
        compiler_params=pltpu.CompilerParams(
            dimension_semantics=("arbitrary",),
            vmem_limit_bytes=VMEM_LIMIT_BYTES),
        name="mix",
    )(x, att, a, u, gg, gl, ga, wo, g2, wg, wu, wd)


def _block_diag(w):
    n, d, e = w.shape
    eye = jnp.eye(n, dtype=w.dtype)
    return jnp.einsum('nde,nm->ndme', w, eye).reshape(n * d, n * e)


def _bias_select():
    npair = ATT_WIDTH // HEAD_PAIR
    src = jnp.arange(LANES)
    piece, h = src // ATT_HEADS, src % ATT_HEADS
    slot = PIECES * (h % 2) + piece
    valid = src < PIECES * ATT_HEADS
    sel_k = ((jnp.arange(ATT_WIDTH)[None, :] == ((h // 2) * HEAD_PAIR + slot)[:, None])
             & valid[:, None])
    sel_q = ((jnp.arange(npair * BIAS_ROWS)[:, None] == ((h // 2) * BIAS_ROWS + PAIR_SLOTS + slot)[None, :])
             & valid[None, :])
    in_pair = jnp.arange(ATT_WIDTH) % HEAD_PAIR
    kone = ((in_pair >= PAIR_SLOTS) & (in_pair < 2 * PAIR_SLOTS)).astype(F32).reshape(1, ATT_WIDTH)
    return sel_k.astype(BF16), sel_q.astype(BF16), kone


def _tiles(S):
    tp = min(1024, S)
    tm = min(512, S)
    tq = min(1024, S)
    td = min(256, S)
    tf = 256
    pr = min(16, S)
    return tp, tm, tq, td, tf, pr


def _layer(x, norm1_g, w_in, q_norm_g, k_norm_g, b_f, conv_w, conv_b, w_a, b_a, w_x, b_x,
           lam, attn_out_g, lru_out_g, w_out, norm2_g, w_gate, w_up, w_down):
    B, S, D = x.shape
    aw = ATT_WIDTH
    tp, tm, tq, td, tf, pr = _tiles(S)
    row = lambda a: a.reshape(1, -1).astype(F32)
    lane_pad = lambda a: jnp.pad(a, ((0, 0), (0, LANES - a.shape[1])))
    per_row = lambda g: jnp.broadcast_to(jnp.tile(g.astype(F32), ATT_HEADS)[:, None], (aw, LANES))

    wqkv = w_in[:, :3 * aw].astype(BF16)
    wf = lane_pad(jnp.tile(w_in[:, 3 * aw:3 * aw + ATT_HEADS], (1, PIECES))).astype(BF16)
    bf = lane_pad(jnp.tile(b_f.astype(F32).reshape(1, ATT_HEADS), (1, PIECES)))
    wlru = w_in[:, 3 * aw + ATT_HEADS:].astype(BF16)
    sel_k, sel_q, kone = _bias_select()

    qt, qe, kx, vt, a, u, gg = _proj_call(
        x, row(norm1_g), wqkv, wf, wlru, per_row(q_norm_g), per_row(k_norm_g), bf, sel_k, sel_q,
        kone, conv_w.astype(F32), row(conv_b), _block_diag(w_a).astype(BF16), row(b_a),
        _block_diag(w_x).astype(BF16), row(b_x), row(lam), tp)
    att = lax.cond(
        _direct_softmax_is_safe(norm1_g, w_in[:, 2 * aw:3 * aw], q_norm_g, k_norm_g),
        functools.partial(_attn_call, tq=tq, tk=tp, td=td, direct=True),
        functools.partial(_attn_call, tq=tq, tk=tp, td=td, direct=False),
        qt, qe, kx, vt)
    return _mix_call(x, att, a, u, gg, row(lru_out_g), row(attn_out_g), w_out.astype(BF16),
                     row(norm2_g), w_gate.astype(BF16), w_up.astype(BF16), w_down.astype(BF16),
                     tm, tf, pr)


def kernel(x, norm1_g, w_in, q_norm_g, k_norm_g, b_f, conv_w, conv_b, w_a, b_a, w_x, b_x, lam,
           attn_out_g, lru_out_g, w_out, norm2_g, w_gate, w_up, w_down):
    depth = norm1_g.shape[0]
    for l in range(depth):
        x = _layer(x, norm1_g[l], w_in[l], q_norm_g[l], k_norm_g[l], b_f[l], conv_w[l],
                   conv_b[l], w_a[l], b_a[l], w_x[l], b_x[l], lam[l], attn_out_g[l],
                   lru_out_g[l], w_out[l], norm2_g[l], w_gate[l], w_up[l], w_down[l])
    return x
```

```python
import functools
import math

import jax
import jax.numpy as jnp
from jax import lax
from jax.experimental import pallas as pl
from jax.experimental.pallas import tpu as pltpu

ATT_HEADS = 8
HEAD_DIM = 64
ATT_WIDTH = ATT_HEADS * HEAD_DIM
CONV_WIDTH = 4
LRU_C = 8.0
NORM_EPS = 1e-6
LOG2E = 1.4426950408889634

LANES = 128
SUBLANES = 8
HEAD_PAIR = 2 * HEAD_DIM
PIECES = 3
PAIR_SLOTS = 2 * PIECES
BIAS_ROWS = 16
GATE_BLOCK = 256
PAIRS_PER_STEP = 2
DIRECT_SOFTMAX_MAX_BITS = 100.0
VMEM_LIMIT_BYTES = 56 * 1024 * 1024

F32 = jnp.float32
BF16 = jnp.bfloat16


def _dot(a, b):
    return jnp.dot(a, b, preferred_element_type=F32)


def _dot_nt(a, b):
    return lax.dot_general(a, b, (((1,), (1,)), ((), ())), preferred_element_type=F32)


def _rms(xf, g):
    return xf * lax.rsqrt(jnp.mean(xf * xf, axis=-1, keepdims=True) + NORM_EPS) * g


def _split3(x):
    hi = x.astype(BF16)
    r = x - hi.astype(F32)
    mid = r.astype(BF16)
    lo = (r - mid.astype(F32)).astype(BF16)
    return hi, mid, lo


def _zero_after(v):
    bits = v[0:SUBLANES, 0:LANES].astype(jnp.int32)
    return lax.shift_right_logical(lax.shift_right_logical(bits, 16), 16).astype(F32)


def _gelu_tanh(x):
    c = math.sqrt(2.0 / math.pi)
    half = 0.5 * x
    return half + half * jnp.tanh(x * (c + (c * 0.044715) * (x * x)))


def _in_group(x):
    rows, C = x.shape
    x3 = x.reshape(rows // SUBLANES, SUBLANES, C)
    return x3, lax.broadcasted_iota(jnp.int32, x3.shape, 1)


def _head_norm_t(t, g_ref):
    tm = t.shape[1]
    outs = []
    for hh in range(ATT_HEADS):
        th = t[hh * HEAD_DIM:(hh + 1) * HEAD_DIM]
        outs.append(th * lax.rsqrt(jnp.mean(th * th, axis=0, keepdims=True) + NORM_EPS))
    return jnp.concatenate(outs, axis=0) * jnp.tile(g_ref[...], (1, tm // LANES))


def _proj_kernel(x_ref, g1_ref, wqkv_ref, wf_ref, wlru_ref, gq_ref, gk_ref, bf_ref,
                 selk_ref, selq_ref, kone_ref, cw_ref, cb_ref, wa_ref, ba_ref, wx_ref, bx_ref,
                 lam_ref,
                 qt_ref, qe_ref, kx_ref, vt_ref, a_ref, u_ref, gg_ref,
                 carry_ref, halo_ref, xc_ref, pa_ref, px_ref,
                 *, tiles_per_seq, n_tiles):
    g = pl.program_id(0)
    tm = x_ref.shape[1]
    C = a_ref.shape[-1]
    aw = ATT_WIDTH
    new, old = g % 2, 1 - g % 2

    @pl.when(g == 0)
    def _():
        for ref in (carry_ref, halo_ref, xc_ref, pa_ref, px_ref):
            ref[...] = jnp.zeros_like(ref)

    repeat = g == n_tiles
    first = (jnp.minimum(g, n_tiles - 1) % tiles_per_seq) == 0
    state = lambda ref: jnp.where(first, 0.0, jnp.where(repeat, ref[1], ref[0]))
    carry_in, halo = state(carry_ref), state(halo_ref)
    carry_ref[1] = carry_in
    halo_ref[1] = halo

    h = _rms(x_ref[0], g1_ref[...]).astype(BF16)

    lx = _dot(h, wlru_ref[:, 0:C])
    f_logit = _dot(h, wf_ref[...])
    q_raw = _dot(h, wqkv_ref[:, 0:aw])
    k_raw = _dot(h, wqkv_ref[:, aw:2 * aw])

    logf = jax.nn.log_sigmoid(f_logit + bf_ref[...])
    c3, rin = _in_group(logf)
    for sh in (1, 2, 4):
        c3 = c3 + jnp.where(rin >= sh, pltpu.roll(c3, sh, axis=1), 0.0)
    prev = carry_in
    groups = []
    for gidx in range(tm // SUBLANES):
        cg = c3[gidx] + prev
        groups.append(cg)
        prev = jnp.broadcast_to(cg[SUBLANES - 1:SUBLANES, :], cg.shape)
    carry_ref[0] = prev
    cum = jnp.concatenate(groups, axis=0)
    hi, mid, lo = _split3(cum * LOG2E)
    lane = lax.broadcasted_iota(jnp.int32, hi.shape, 1)
    pieces = jnp.where(lane < ATT_HEADS, hi, jnp.where(lane < 2 * ATT_HEADS, mid, lo))

    x3, rin = _in_group(lx)
    halo_ref[0] = lx[tm - SUBLANES:tm, :]
    xc3 = cb_ref[...] + x3 * cw_ref[CONV_WIDTH - 1:CONV_WIDTH, :]
    for k in range(1, CONV_WIDTH):
        rolled = pltpu.roll(x3, k, axis=1)
        before = jnp.concatenate([pltpu.roll(halo, k, axis=0)[None], rolled[:-1]], axis=0)
        tap = CONV_WIDTH - 1 - k
        xc3 = xc3 + jnp.where(rin >= k, rolled, before) * cw_ref[tap:tap + 1, :]
    xc = xc3.reshape(tm, C)
    xc_ref[new] = xc
    xcb = xc.astype(BF16)

    nparts = 8
    decay_rate = -LRU_C * jax.nn.softplus(-lam_ref[...])
    for part in range(nparts):
        rows = slice(part * (tm // nparts), (part + 1) * (tm // nparts))
        log_a = jax.nn.sigmoid(pa_ref[old, rows, :]) * decay_rate
        gi = jax.nn.sigmoid(px_ref[old, rows, :])
        a = jnp.exp(log_a)
        a_ref[0, rows, :] = a
        y = -jnp.tanh(log_a) * (1.0 + a * a)
        root = jnp.where(y > 0.0, y * lax.rsqrt(y), 0.0)
        u_ref[0, rows, :] = root * (gi * xc_ref[old, rows, :])

    v_raw = _dot(h, wqkv_ref[:, 2 * aw:3 * aw])
    ext_k = (_dot(pieces, selk_ref[...]) + kone_ref[...]).astype(BF16)
    qe_ref[0] = _dot_nt(selq_ref[...], pieces).astype(BF16)
    gg_ref[0] = _dot(h, wlru_ref[:, C:2 * C]).astype(BF16)
    for w_ref, b_ref, p_ref in ((wa_ref, ba_ref, pa_ref), (wx_ref, bx_ref, px_ref)):
        for n in range(C // GATE_BLOCK):
            blk = slice(n * GATE_BLOCK, (n + 1) * GATE_BLOCK)
            p_ref[new, :, blk] = _dot(xcb[:, blk], w_ref[blk, blk]) + b_ref[:, blk]

    q = _head_norm_t(q_raw.T, gq_ref) * (LOG2E / math.sqrt(HEAD_DIM))
    qt_ref[0] = q.astype(BF16)
    k = _head_norm_t(k_raw.T, gk_ref).T.astype(BF16)
    vt_ref[0, 0] = v_raw.T.astype(BF16).reshape(vt_ref.shape[2:])
    for p in range(aw // HEAD_PAIR):
        pair = slice(p * HEAD_PAIR, (p + 1) * HEAD_PAIR)
        kx_ref[0, :, 2 * p * HEAD_PAIR:(2 * p + 1) * HEAD_PAIR] = k[:, pair]
        kx_ref[0, :, (2 * p + 1) * HEAD_PAIR:(2 * p + 2) * HEAD_PAIR] = ext_k[:, pair]


def _proj_call(x, g1, wqkv, wf, wlru, gq, gk, bf, sel_k, sel_q, kone, cw, cb, wa, ba, wx, bx, lam,
               tm):
    B, S, D = x.shape
    aw = ATT_WIDTH
    npair = aw // HEAD_PAIR
    C = wlru.shape[1] // 2
    nS = S // tm
    N = B * nS
    const = lambda shape: pl.BlockSpec(shape, lambda g: (0,) * len(shape))
    cur = lambda g: (jnp.minimum(g, N - 1) // nS, jnp.minimum(g, N - 1) % nS)
    prev = lambda g: (jnp.maximum(g - 1, 0) // nS, jnp.maximum(g - 1, 0) % nS)
    tok = lambda w, tile: pl.BlockSpec((1, tm, w), lambda g: (*tile(g), 0))
    return pl.pallas_call(
        functools.partial(_proj_kernel, tiles_per_seq=nS, n_tiles=N),
        grid=(N + 1,),
        in_specs=[tok(D, cur), const((1, D)), const(wqkv.shape), const(wf.shape),
                  const(wlru.shape), const((aw, LANES)), const((aw, LANES)), const((1, LANES)),
                  const(sel_k.shape), const(sel_q.shape), const((1, aw)),
                  const(cw.shape), const((1, C)), const(wa.shape), const((1, C)),
                  const(wx.shape), const((1, C)), const((1, C))],
        out_specs=[pl.BlockSpec((1, aw, tm), lambda g: (cur(g)[0], 0, cur(g)[1])),
                   pl.BlockSpec((1, npair * BIAS_ROWS, tm), lambda g: (cur(g)[0], 0, cur(g)[1])),
                   tok(2 * aw, cur),
                   pl.BlockSpec((1, 1, npair, HEAD_PAIR, tm), lambda g: (*cur(g), 0, 0, 0)),
                   tok(C, prev), tok(C, prev), tok(C, cur)],
        out_shape=[jax.ShapeDtypeStruct((B, aw, S), BF16),
                   jax.ShapeDtypeStruct((B, npair * BIAS_ROWS, S), BF16),
                   jax.ShapeDtypeStruct((B, S, 2 * aw), BF16),
                   jax.ShapeDtypeStruct((B, S // tm, npair, HEAD_PAIR, tm), BF16),
                   jax.ShapeDtypeStruct((B, S, C), F32),
                   jax.ShapeDtypeStruct((B, S, C), F32),
                   jax.ShapeDtypeStruct((B, S, C), BF16)],
        scratch_shapes=[pltpu.VMEM((2, SUBLANES, LANES), F32), pltpu.VMEM((2, SUBLANES, C), F32)]
        + [pltpu.VMEM((2, tm, C), F32)] * 3,
        compiler_params=pltpu.CompilerParams(
            dimension_semantics=("arbitrary",),
            vmem_limit_bytes=VMEM_LIMIT_BYTES),
        name="proj",
    )(x, g1, wqkv, wf, wlru, gq, gk, bf, sel_k, sel_q, kone, cw, cb, wa, ba, wx, bx, lam)


def _stacked_queries(qt, qe, direct):
    tq = qt.shape[1]
    zeros = jnp.zeros((HEAD_DIM, tq), BF16)
    pad = jnp.zeros((HEAD_PAIR - BIAS_ROWS, tq), BF16)
    erow = lax.broadcasted_iota(jnp.int32, (BIAS_ROWS, tq), 0)

    def bias_rows(h):
        lo = PIECES * h
        rows = jnp.where((erow >= lo) & (erow < lo + PIECES), -1.0, 0.0).astype(BF16)
        if direct:
            rows = jnp.where((erow >= PAIR_SLOTS + lo) & (erow < PAIR_SLOTS + lo + PIECES), qe, rows)
        return rows

    return (jnp.concatenate([qt[0:HEAD_DIM], zeros, bias_rows(0), pad], axis=0),
            jnp.concatenate([zeros, qt[HEAD_DIM:HEAD_PAIR], bias_rows(1), pad], axis=0))


def _softmax_step(state, s, vb, head, direct):
    m, l, acc = state
    rows = slice(head * HEAD_DIM, (head + 1) * HEAD_DIM)
    if direct:
        p = jnp.exp2(s)
        l = l + jnp.sum(p, axis=0, keepdims=True)
        return None, l, acc + _dot(vb, p.astype(BF16))[rows]
    m_new = jnp.maximum(m, jnp.max(s, axis=0, keepdims=True))
    alpha = jnp.exp2(m - m_new)
    p = jnp.exp2(s - m_new)
    l = alpha * l + jnp.sum(p, axis=0, keepdims=True)
    return m_new, l, alpha * acc + _dot(vb, p.astype(BF16))[rows]


def _attn_tile(i, qt_ref, qe_ref, kx_ref, vt_ref, o_ref, *, tq, tk, td, direct):
    qs = _stacked_queries(qt_ref[0], qe_ref[0], direct)
    qst = jnp.concatenate(qs, axis=1)
    nk = tq // tk

    def full_scores(j):
        return _dot(kx_ref[0, j * tq:(j + 1) * tq, :], qst)

    def diag_scores():
        return [_dot(kx_ref[0, i * tq + c * td:i * tq + (c + 1) * td, :],
                     jnp.concatenate([qs[0][:, c * td:], qs[1][:, c * td:]], axis=1))
                for c in range(tq // td)]

    carry = tuple((None if direct else jnp.full((1, tq), -jnp.inf, F32), jnp.zeros((1, tq), F32),
                   jnp.zeros((HEAD_DIM, tq), F32)) for _ in range(2))
    ahead = full_scores(0) if i else diag_scores()
    for j in range(i):
        s2 = ahead
        ahead = full_scores(j + 1) if j + 1 < i else diag_scores()
        vb = jnp.concatenate([vt_ref[0, j * nk + n, 0] for n in range(nk)], axis=1)
        carry = tuple(_softmax_step(carry[h], s2[:, h * tq:(h + 1) * tq], vb, h, direct)
                      for h in range(2))

    krow = lax.broadcasted_iota(jnp.int32, (td, td), 0)
    qcol = lax.broadcasted_iota(jnp.int32, (td, td), 1)
    keep = krow <= qcol
    for c, s2 in enumerate(ahead):
        c0 = c * td
        w = tq - c0
        lo = c0 % tk
        vb = vt_ref[0, i * nk + c0 // tk, 0, :, lo:lo + td]
        new = []
        for h in range(2):
            s = s2[:, h * w:(h + 1) * w]
            sd = jnp.where(keep, s[:, 0:td], -jnp.inf)
            s = sd if w == td else jnp.concatenate([sd, s[:, td:]], axis=1)
            old = carry[h]
            part = _softmax_step(tuple(None if t is None else t[:, c0:] for t in old), s, vb, h,
                                 direct)
            if c0:
                part = tuple(None if t is None else jnp.concatenate([t[:, 0:c0], u], axis=1)
                             for t, u in zip(old, part))
            new.append(part)
        carry = tuple(new)

    (_, la, acca), (_, lb, accb) = carry
    ot = jnp.concatenate([acca / la, accb / lb], axis=0)
    o_ref[0] = ot.T.astype(o_ref.dtype)


def _attn_kernel(direct_ref, qt_ref, qe_ref, kx_ref, vt_ref, o_ref, *, n_tiles, **static):
    def tile(i, direct):
        for pp in range(PAIRS_PER_STEP):
            rows = lambda n: pl.ds(pp * n, n)
            _attn_tile(i, qt_ref.at[:, rows(HEAD_PAIR), :], qe_ref.at[:, rows(BIAS_ROWS), :],
                       kx_ref.at[:, :, rows(2 * HEAD_PAIR)], vt_ref.at[:, :, rows(1)],
                       o_ref.at[:, :, rows(HEAD_PAIR)], direct=direct, **static)

    for direct in (True, False):
        for i in range(n_tiles):
            pl.when((pl.program_id(2) == i) & (direct_ref[0] == int(direct)))(
                functools.partial(tile, i, direct))


def _attn_call(direct, qt, qe, kx, vt, tq, tk, td):
    B, aw, S = qt.shape
    npair = aw // HEAD_PAIR
    pps = PAIRS_PER_STEP
    assert vt.shape == (B, S // tk, npair, HEAD_PAIR, tk) and npair % pps == 0
    return pl.pallas_call(
        functools.partial(_attn_kernel, n_tiles=S // tq, tq=tq, tk=tk, td=td),
        grid_spec=pltpu.PrefetchScalarGridSpec(
            num_scalar_prefetch=1,
            grid=(B, npair // pps, S // tq),
            in_specs=[pl.BlockSpec((1, pps * HEAD_PAIR, tq), lambda b, p, i, d: (b, p, i)),
                      pl.BlockSpec((1, pps * BIAS_ROWS, tq), lambda b, p, i, d: (b, p, i)),
                      pl.BlockSpec((1, S, pps * 2 * HEAD_PAIR), lambda b, p, i, d: (b, 0, p)),
                      pl.BlockSpec((1, S // tk, pps, HEAD_PAIR, tk),
                                   lambda b, p, i, d: (b, 0, p, 0, 0))],
            out_specs=pl.BlockSpec((1, tq, pps * HEAD_PAIR), lambda b, p, i, d: (b, i, p))),
        out_shape=jax.ShapeDtypeStruct((B, S, aw), BF16),
        compiler_params=pltpu.CompilerParams(
            dimension_semantics=("arbitrary", "arbitrary", "arbitrary"),
            vmem_limit_bytes=VMEM_LIMIT_BYTES),
        name="attn",
    )(direct, qt, qe, kx, vt)


def _direct_softmax_is_safe(norm1_g, w_v, q_norm_g, k_norm_g):
    d_model = norm1_g.shape[0]
    qk_bits = LOG2E * math.sqrt(HEAD_DIM) * jnp.max(jnp.abs(q_norm_g)) * jnp.max(jnp.abs(k_norm_g))
    v_bound = (math.sqrt(d_model) * jnp.max(jnp.abs(norm1_g))
               * jnp.max(jnp.sqrt(jnp.sum(jnp.square(w_v.astype(F32)), axis=0))))
    return qk_bits + jnp.log2(jnp.maximum(v_bound, 1.0)) < DIRECT_SOFTMAX_MAX_BITS


def _lru_piece(rows, hprev, a_ref, u_ref, gg_ref, g_ref, rec_ref, after):
    a = a_ref[0, rows, :]
    u = u_ref[0, rows, :]
    pr, C = a.shape
    if after is not None:
        u = u + jnp.tile(_zero_after(after), (pr // SUBLANES, C // LANES))

    a, rin = _in_group(a)
    u, _ = _in_group(u)
    for sh in (1, 2, 4):
        ok = rin >= sh
        a_prev = jnp.where(ok, pltpu.roll(a, sh, axis=1), 1.0)
        u_prev = jnp.where(ok, pltpu.roll(u, sh, axis=1), 0.0)
        u = a * u_prev + u
        a = a * a_prev

    hs = []
    for gidx in range(pr // SUBLANES):
        hg = a[gidx] * hprev + u[gidx]
        hs.append(hg)
        hprev = jnp.broadcast_to(hg[SUBLANES - 1:SUBLANES, :], hg.shape)
    rec = jnp.concatenate(hs, axis=0) * _gelu_tanh(gg_ref[0, rows, :].astype(F32))
    rec_ref[rows, :] = _rms(rec, g_ref[...]).astype(BF16)
    return hprev


def _mix_kernel(x_ref, att_ref, a_ref, u_ref, gg_ref, gl_ref, ga_ref, wo_ref, g2_ref, wg_ref,
                wu_ref, wd_ref, o_ref, act_ref, rec_ref, hc_ref, x1_ref, h2_ref,
                *, tf, pr, tiles_per_seq, n_tiles):
    g = pl.program_id(0)
    new, old = g % 2, 1 - g % 2

    @pl.when(g == 0)
    def _():
        for ref in (hc_ref, x1_ref, h2_ref):
            ref[...] = jnp.zeros_like(ref)

    h2 = h2_ref[...]
    first = (jnp.minimum(g, n_tiles - 1) % tiles_per_seq) == 0
    hprev = jnp.where(first, 0.0, hc_ref[...])

    tm = x_ref.shape[1]
    dff = wg_ref.shape[1]
    prev_act = None
    nchunk, npiece = dff // tf, tm // pr
    for c in range(nchunk):
        cols = slice(c * tf, (c + 1) * tf)
        act = jax.nn.silu(_dot(h2, wg_ref[:, cols])) * _dot(h2, wu_ref[:, cols])
        act_ref[:, cols] = act.astype(BF16)
        for p in range(npiece):
            if p * nchunk // npiece == c:
                hprev = _lru_piece(slice(p * pr, (p + 1) * pr), hprev, a_ref, u_ref, gg_ref,
                                   gl_ref, rec_ref, prev_act)
        prev_act = act
    hc_ref[...] = hprev

    aw = att_ref.shape[-1]
    attn = _rms(att_ref[0].astype(F32), ga_ref[...]).astype(BF16)
    x1 = x_ref[0] + _dot(attn, wo_ref[0:aw, :]) + _dot(rec_ref[...], wo_ref[aw:, :])
    x1_ref[new] = x1
    h2_ref[...] = _rms(x1, g2_ref[...]).astype(BF16)

    o_ref[0] = x1_ref[old] + _dot(act_ref[...], wd_ref[...])


def _mix_call(x, att, a, u, gg, gl, ga, wo, g2, wg, wu, wd, tm, tf, pr):
    B, S, D = x.shape
    aw = att.shape[-1]
    C = a.shape[-1]
    nS = S // tm
    N = B * nS
    const = lambda shape: pl.BlockSpec(shape, lambda g: (0,) * len(shape),
                                       pipeline_mode=pl.Buffered(1))

    def cur(w):
        return pl.BlockSpec((1, tm, w), lambda g: (jnp.minimum(g, N - 1) // nS,
                                                   jnp.minimum(g, N - 1) % nS, 0))

    def prev(w):
        return pl.BlockSpec((1, tm, w), lambda g: (jnp.maximum(g - 1, 0) // nS,
                                                   jnp.maximum(g - 1, 0) % nS, 0))

    return pl.pallas_call(
        functools.partial(_mix_kernel, tf=tf, pr=pr, tiles_per_seq=nS, n_tiles=N),
        grid=(N + 1,),
        in_specs=[cur(D), cur(aw), cur(C), cur(C), cur(C), const((1, C)), const((1, aw)),
                  const(wo.shape), const((1, D)), const(wg.shape), const(wu.shape),
                  const(wd.shape)],
        out_specs=prev(D),
        out_shape=jax.ShapeDtypeStruct((B, S, D), x.dtype),
        scratch_shapes=[pltpu.VMEM((tm, wg.shape[1]), BF16), pltpu.VMEM((tm, C), BF16),
                        pltpu.VMEM((SUBLANES, C), F32), pltpu.VMEM((2, tm, D), F32),
                        pltpu.VMEM((tm, D), BF16)],
        compiler_params=pltpu.CompilerParams(
            dimension_semantics=("arbitrary",),
            vmem_limit_bytes=VMEM_LIMIT_BYTES),
        name="mix",
    )(x, att, a, u, gg, gl, ga, wo, g2, wg, wu, wd)


def _block_diag(w):
    n, d, e = w.shape
    eye = jnp.eye(n, dtype=w.dtype)
    return jnp.einsum('nde,nm->ndme', w, eye).reshape(n * d, n * e)


def _bias_select():
    npair = ATT_WIDTH // HEAD_PAIR
    src = jnp.arange(LANES)
    piece, h = src // ATT_HEADS, src % ATT_HEADS
    slot = PIECES * (h % 2) + piece
    valid = src < PIECES * ATT_HEADS
    sel_k = ((jnp.arange(ATT_WIDTH)[None, :] == ((h // 2) * HEAD_PAIR + slot)[:, None])
             & valid[:, None])
    sel_q = ((jnp.arange(npair * BIAS_ROWS)[:, None] == ((h // 2) * BIAS_ROWS + PAIR_SLOTS + slot)[None, :])
             & valid[None, :])
    in_pair = jnp.arange(ATT_WIDTH) % HEAD_PAIR
    kone = ((in_pair >= PAIR_SLOTS) & (in_pair < 2 * PAIR_SLOTS)).astype(F32).reshape(1, ATT_WIDTH)
    return sel_k.astype(BF16), sel_q.astype(BF16), kone


def _tiles(S):
    tp = min(1024, S)
    tm = min(512, S)
    tq = min(1024, S)
    td = min(256, S)
    tf = 256
    pr = min(16, S)
    return tp, tm, tq, td, tf, pr


def _layer(x, norm1_g, w_in, q_norm_g, k_norm_g, b_f, conv_w, conv_b, w_a, b_a, w_x, b_x,
           lam, attn_out_g, lru_out_g, w_out, norm2_g, w_gate, w_up, w_down):
    B, S, D = x.shape
    aw = ATT_WIDTH
    tp, tm, tq, td, tf, pr = _tiles(S)
    row = lambda a: a.reshape(1, -1).astype(F32)
    lane_pad = lambda a: jnp.pad(a, ((0, 0), (0, LANES - a.shape[1])))
    per_row = lambda g: jnp.broadcast_to(jnp.tile(g.astype(F32), ATT_HEADS)[:, None], (aw, LANES))

    wqkv = w_in[:, :3 * aw].astype(BF16)
    wf = lane_pad(jnp.tile(w_in[:, 3 * aw:3 * aw + ATT_HEADS], (1, PIECES))).astype(BF16)
    bf = lane_pad(jnp.tile(b_f.astype(F32).reshape(1, ATT_HEADS), (1, PIECES)))
    wlru = w_in[:, 3 * aw + ATT_HEADS:].astype(BF16)
    sel_k, sel_q, kone = _bias_select()

    qt, qe, kx, vt, a, u, gg = _proj_call(
        x, row(norm1_g), wqkv, wf, wlru, per_row(q_norm_g), per_row(k_norm_g), bf, sel_k, sel_q,
        kone, conv_w.astype(F32), row(conv_b), _block_diag(w_a).astype(BF16), row(b_a),
        _block_diag(w_x).astype(BF16), row(b_x), row(lam), tp)
    direct = _direct_softmax_is_safe(norm1_g, w_in[:, 2 * aw:3 * aw], q_norm_g, k_norm_g)
    att = _attn_call(direct.astype(jnp.int32).reshape(1), qt, qe, kx, vt, tq, tp, td)
    return _mix_call(x, att, a, u, gg, row(lru_out_g), row(attn_out_g), w_out.astype(BF16),
                     row(norm2_g), w_gate.astype(BF16), w_up.astype(BF16), w_down.astype(BF16),
                     tm, tf, pr)


def kernel(x, norm1_g, w_in, q_norm_g, k_norm_g, b_f, conv_w, conv_b, w_a, b_a, w_x, b_x, lam,
           attn_out_g, lru_out_g, w_out, norm2_g, w_gate, w_up, w_down):
    depth = norm1_g.shape[0]
    for l in range(depth):
        x = _layer(x, norm1_g[l], w_in[l], q_norm_g[l], k_norm_g[l], b_f[l], conv_w[l],
                   conv_b[l], w_a[l], b_a[l], w_x[l], b_x[l], lam[l], attn_out_g[l],
                   lru_out_g[l], w_out[l], norm2_g[l], w_gate[l], w_up[l], w_down[l])
    return x
```
